```python
import jax, jax.numpy as jnp
from jax import lax
import numpy as np

D_MODEL = 1024
BATCH = 16
SEQ = 2048
DEPTH = 2

CHUNK = 64
N_MIXERS = 2
HEAD_DIM = 64
MEM_TOKENS = 256
MEM_HEADS = 4
MEM_WIDTH = MEM_HEADS * HEAD_DIM
TOK_WIDTH = D_MODEL - MEM_WIDTH
ATT_HEADS = TOK_WIDTH // HEAD_DIM
LEFT_CHUNKS = 8
BAND = (LEFT_CHUNKS + 1) * CHUNK
BAND_PAD = LEFT_CHUNKS * CHUNK
REL_CLIP = 128
N_REL = REL_CLIP + CHUNK
CONV_WIDTH = 31
CONV_CH = TOK_WIDTH
A_IN = 3 * TOK_WIDTH + MEM_WIDTH
B_IN = 2 * CONV_CH + MEM_WIDTH
D_FF = -(-8 * D_MODEL // (3 * 256)) * 256
EPS = 1e-6
NEG_INF = -1e30
ATTN_SCALE = HEAD_DIM ** -0.5

_DIST = np.arange(CHUNK)[:, None] - np.arange(BAND)[None, :] + BAND_PAD
REL_IDX = np.clip(_DIST, -(CHUNK - 1), REL_CLIP) + (CHUNK - 1)
BAND_OFF = np.arange(BAND) - BAND_PAD

kernel_name = "hybrid_chunkattn_conformerconv_memxattn"


def rms_norm(x, g):
    xf = x.astype(jnp.float32)
    y = xf * lax.rsqrt(jnp.mean(xf * xf, axis=-1, keepdims=True) + EPS)
    return (y * g.astype(jnp.float32)).astype(x.dtype)


def layer_norm(x, g, b):
    xf = x.astype(jnp.float32)
    mu = jnp.mean(xf, axis=-1, keepdims=True)
    xc = xf - mu
    y = xc * lax.rsqrt(jnp.mean(xc * xc, axis=-1, keepdims=True) + EPS)
    return (y * g.astype(jnp.float32) + b.astype(jnp.float32)).astype(x.dtype)


def chunk_relpos_attention(q, k, v, rel_bias):
    B, S, H, Dh = q.shape
    nc = S // CHUNK
    kp = jnp.pad(k, ((0, 0), (BAND_PAD, 0), (0, 0), (0, 0)))
    vp = jnp.pad(v, ((0, 0), (BAND_PAD, 0), (0, 0), (0, 0)))
    qc = q.reshape(B, nc, CHUNK, H, Dh).transpose(1, 0, 2, 3, 4)
    bias = rel_bias[:, REL_IDX].astype(jnp.float32)
    band_off = jnp.asarray(BAND_OFF, dtype=jnp.int32)

    def one_chunk(args):
        c, q_c = args
        start = c * CHUNK
        k_b = lax.dynamic_slice_in_dim(kp, start, BAND, axis=1)
        v_b = lax.dynamic_slice_in_dim(vp, start, BAND, axis=1)
        s = jnp.einsum('bqhd,bkhd->bhqk', q_c, k_b).astype(jnp.float32) * ATTN_SCALE + bias
        valid = (start + band_off) >= 0
        s = jnp.where(valid, s, NEG_INF)
        p = jax.nn.softmax(s, axis=-1).astype(v.dtype)
        return jnp.einsum('bhqk,bkhd->bqhd', p, v_b)

    out = lax.map(one_chunk, (jnp.arange(nc, dtype=jnp.int32), qc))
    return out.transpose(1, 0, 2, 3, 4).reshape(B, S, H * Dh)


def conformer_conv(u, conv_w, conv_b, ln_g, ln_b):
    a, gate = jnp.split(u, 2, axis=-1)
    h = a * jax.nn.sigmoid(gate)
    hp = jnp.pad(h, ((0, 0), (CONV_WIDTH - 1, 0), (0, 0)))
    y = lax.conv_general_dilated(
        hp, conv_w[:, None, :].astype(h.dtype), window_strides=(1,), padding='VALID',
        dimension_numbers=('NWC', 'WIO', 'NWC'), feature_group_count=CONV_CH)
    y = y + conv_b
    return jax.nn.silu(layer_norm(y, ln_g, ln_b))


def memory_attention(qm, mem_n, w_mem_kv, mq_g, mk_g):
    B, S, _ = qm.shape
    M = mem_n.shape[1]
    km, vm = jnp.split(mem_n @ w_mem_kv, 2, axis=-1)
    q = rms_norm(qm.reshape(B, S, MEM_HEADS, HEAD_DIM), mq_g)
    k = rms_norm(km.reshape(B, M, MEM_HEADS, HEAD_DIM), mk_g)
    v = vm.reshape(B, M, MEM_HEADS, HEAD_DIM)
    s = jnp.einsum('bshd,bmhd->bhsm', q, k).astype(jnp.float32) * ATTN_SCALE
    p = jax.nn.softmax(s, axis=-1).astype(v.dtype)
    return jnp.einsum('bhsm,bmhd->bshd', p, v).reshape(B, S, MEM_WIDTH)


def setup_inputs(seed: int = 0) -> dict:
    key = jax.random.key(seed)
    ks = jax.random.split(key, 22)
    n_a = (DEPTH + N_MIXERS - 1) // N_MIXERS
    n_b = DEPTH // N_MIXERS
    f32 = jnp.float32

    def w(k, shape, fan_in):
        return jax.random.normal(k, shape, f32) * (fan_in ** -0.5)

    def gain(k, shape):
        return 1.0 + 0.05 * jax.random.normal(k, shape, f32)

    def small(k, shape, s=0.02):
        return s * jax.random.normal(k, shape, f32)

    return {
        "x": jax.random.normal(ks[0], (BATCH, SEQ, D_MODEL), f32),
        "mem": jax.random.normal(ks[1], (BATCH, MEM_TOKENS, D_MODEL), f32),
        "norm1_g": gain(ks[2], (DEPTH, D_MODEL)),
        "mem_norm_g": gain(ks[3], (DEPTH, D_MODEL)),
        "a_w_in": w(ks[4], (n_a, D_MODEL, A_IN), D_MODEL),
        "a_q_g": gain(ks[5], (n_a, HEAD_DIM)),
        "a_k_g": gain(ks[6], (n_a, HEAD_DIM)),
        "a_rel_bias": small(ks[7], (n_a, ATT_HEADS, N_REL), 0.3),
        "b_w_in": w(ks[8], (n_b, D_MODEL, B_IN), D_MODEL),
        "b_b_in": small(ks[9], (n_b, B_IN)),
        "b_conv_w": w(ks[10], (n_b, CONV_WIDTH, CONV_CH), CONV_WIDTH),
        "b_conv_b": small(ks[11], (n_b, CONV_CH)),
        "b_ln_g": gain(ks[12], (n_b, CONV_CH)),
        "b_ln_b": small(ks[13], (n_b, CONV_CH)),
        "mq_g": gain(ks[14], (DEPTH, HEAD_DIM)),
        "mk_g": gain(ks[15], (DEPTH, HEAD_DIM)),
        "w_mem_kv": w(ks[16], (DEPTH, D_MODEL, 2 * MEM_WIDTH), D_MODEL),
        "w_out": w(ks[17], (DEPTH, D_MODEL, D_MODEL), D_MODEL),
        "norm2_g": gain(ks[18], (DEPTH, D_MODEL)),
        "w_gate": w(ks[19], (DEPTH, D_MODEL, D_FF), D_MODEL),
        "w_up": w(ks[20], (DEPTH, D_MODEL, D_FF), D_MODEL),
        "w_down": w(ks[21], (DEPTH, D_FF, D_MODEL), D_FF),
    }


def reference(x, mem, norm1_g, mem_norm_g, a_w_in, a_q_g, a_k_g, a_rel_bias,
              b_w_in, b_b_in, b_conv_w, b_conv_b, b_ln_g, b_ln_b,
              mq_g, mk_g, w_mem_kv, w_out, norm2_g, w_gate, w_up, w_down):
    B, S, _ = x.shape
    for i in range(DEPTH):
        j = i // N_MIXERS
        h = rms_norm(x, norm1_g[i])
        mem_n = rms_norm(mem, mem_norm_g[i])
        if i % N_MIXERS == 0:
            z = h @ a_w_in[j]
            q, k, v, qm = jnp.split(z, [TOK_WIDTH, 2 * TOK_WIDTH, 3 * TOK_WIDTH], axis=-1)
            q = rms_norm(q.reshape(B, S, ATT_HEADS, HEAD_DIM), a_q_g[j])
            k = rms_norm(k.reshape(B, S, ATT_HEADS, HEAD_DIM), a_k_g[j])
            v = v.reshape(B, S, ATT_HEADS, HEAD_DIM)
            tok = chunk_relpos_attention(q, k, v, a_rel_bias[j])
        else:
            z = h @ b_w_in[j] + b_b_in[j]
            u, qm = jnp.split(z, [2 * CONV_CH], axis=-1)
            tok = conformer_conv(u, b_conv_w[j], b_conv_b[j], b_ln_g[j], b_ln_b[j])
        memo = memory_attention(qm, mem_n, w_mem_kv[i], mq_g[i], mk_g[i])
        x = x + jnp.concatenate([tok, memo], axis=-1) @ w_out[i]
        h2 = rms_norm(x, norm2_g[i])
        x = x + (jax.nn.silu(h2 @ w_gate[i]) * (h2 @ w_up[i])) @ w_down[i]
    return x
```

```python
import functools

import numpy as np
import jax
import jax.numpy as jnp
from jax import lax
from jax.experimental import pallas as pl
from jax.experimental.pallas import tpu as pltpu

D_MODEL = 1024
CHUNK = 64
HEAD_DIM = 64
MEM_TOKENS = 256
MEM_HEADS = 4
MEM_WIDTH = MEM_HEADS * HEAD_DIM
TOK_WIDTH = D_MODEL - MEM_WIDTH
ATT_HEADS = TOK_WIDTH // HEAD_DIM
LEFT_CHUNKS = 8
BAND = (LEFT_CHUNKS + 1) * CHUNK
BAND_PAD = LEFT_CHUNKS * CHUNK
REL_CLIP = 128
CONV_WIDTH = 31
CONV_CH = TOK_WIDTH
A_IN = 3 * TOK_WIDTH + MEM_WIDTH
B_IN = 2 * CONV_CH + MEM_WIDTH
D_FF = 2816
EPS = 1e-6
NEG_INF = -1e30
ATTN_SCALE = HEAD_DIM ** -0.5

LANES = 128
PAIR = 2 * HEAD_DIM
TOK_TILE = 512
ATT_TILE = BAND_PAD
COL_CHUNK = 256
FF_CHUNK = 256
CONV_TILE = 256
CONV_HALO = 32
CONV_ROWS = 64
VMEM_LIMIT = 56 * 1024 * 1024

_DIST = np.arange(CHUNK)[:, None] - np.arange(BAND)[None, :] + BAND_PAD
REL_IDX = np.clip(_DIST, -(CHUNK - 1), REL_CLIP) + (CHUNK - 1)

BF16 = jnp.bfloat16
F32 = jnp.float32


def _const_spec(shape):
    return pl.BlockSpec(shape, lambda *_: (0,) * len(shape), pipeline_mode=pl.Buffered(1))


def _rms(x, g):
    return x * lax.rsqrt(jnp.mean(x * x, axis=-1, keepdims=True) + EPS) * g


def _left_mask(shape):
    return lax.broadcasted_iota(jnp.int32, shape, len(shape) - 1) < HEAD_DIM


def _pair_head_norm(z, g):
    left = _left_mask(z.shape)
    sq = z * z
    ss_l = jnp.sum(jnp.where(left, sq, 0.0), axis=-1, keepdims=True)
    ss_r = jnp.sum(jnp.where(left, 0.0, sq), axis=-1, keepdims=True)
    r = jnp.where(left, lax.rsqrt(ss_l * (1.0 / HEAD_DIM) + EPS), lax.rsqrt(ss_r * (1.0 / HEAD_DIM) + EPS))
    return z * r * g


def _head_norm(z, g):
    parts = [_pair_head_norm(z[:, p * LANES:(p + 1) * LANES], g[:, p * LANES:(p + 1) * LANES])
             for p in range(z.shape[1] // LANES)]
    return jnp.concatenate(parts, axis=1) if len(parts) > 1 else parts[0]


def _stack_pair(q):
    left = _left_mask(q.shape)
    zero = jnp.zeros_like(q)
    return jnp.concatenate([jnp.where(left, q, zero), jnp.where(left, zero, q)], axis=0)


def _unstack_pair(o2):
    rows = o2.shape[0] // 2
    left = _left_mask((rows, LANES))
    return jnp.where(left, o2[:rows], o2[rows:])


def _nt_dot(a, b):
    return lax.dot_general(a, b, (((1,), (1,)), ((), ())), preferred_element_type=F32)


def _mem_kv_kernel(mem_ref, g_ref, w_ref, gk_ref, k_ref, v_ref):
    m = _rms(mem_ref[...], g_ref[...]).astype(BF16)
    kv = jnp.dot(m, w_ref[...], preferred_element_type=F32)
    k_ref[...] = _head_norm(kv[:, :MEM_WIDTH], gk_ref[...]).astype(BF16)
    v_ref[...] = kv[:, MEM_WIDTH:].astype(BF16)


def _mem_kv(mem, g, w, gk):
    B = mem.shape[0]
    return pl.pallas_call(
        _mem_kv_kernel,
        grid=(B,),
        in_specs=[
            pl.BlockSpec((None, MEM_TOKENS, D_MODEL), lambda b: (b, 0, 0)),
            _const_spec((1, D_MODEL)),
            _const_spec((D_MODEL, 2 * MEM_WIDTH)),
            _const_spec((1, MEM_WIDTH)),
        ],
        out_specs=[
            pl.BlockSpec((None, MEM_TOKENS, MEM_WIDTH), lambda b: (b, 0, 0)),
            pl.BlockSpec((None, MEM_TOKENS, MEM_WIDTH), lambda b: (b, 0, 0)),
        ],
        out_shape=[jax.ShapeDtypeStruct((B, MEM_TOKENS, MEM_WIDTH), BF16)] * 2,
        compiler_params=pltpu.CompilerParams(dimension_semantics=("arbitrary",)),
        name="mem_kv",
    )(mem, g, w, gk)


def _in_a_kernel(x_ref, g1_ref, w_ref, gq_ref, gk_ref, gm_ref, o_ref):
    h = _rms(x_ref[...], g1_ref[...]).astype(BF16)
    n_tok = TOK_WIDTH // COL_CHUNK
    for j in range(A_IN // COL_CHUNK):
        cols = slice(j * COL_CHUNK, (j + 1) * COL_CHUNK)
        z = jnp.dot(h, w_ref[:, cols], preferred_element_type=F32)
        if j < n_tok:
            z = _head_norm(z, gq_ref[...])
        elif j < 2 * n_tok:
            z = _head_norm(z, gk_ref[...])
        elif j >= 3 * n_tok:
            z = _head_norm(z, gm_ref[...])
        o_ref[:, cols] = z.astype(BF16)


def _in_proj_a(x2, g1, w, gq, gk, gm):
    T = x2.shape[0]
    return pl.pallas_call(
        _in_a_kernel,
        grid=(T // TOK_TILE,),
        in_specs=[
            pl.BlockSpec((TOK_TILE, D_MODEL), lambda i: (i, 0)),
            _const_spec((1, D_MODEL)),
            _const_spec((D_MODEL, A_IN)),
            _const_spec((1, COL_CHUNK)),
            _const_spec((1, COL_CHUNK)),
            _const_spec((1, COL_CHUNK)),
        ],
        out_specs=pl.BlockSpec((TOK_TILE, A_IN), lambda i: (i, 0)),
        out_shape=jax.ShapeDtypeStruct((T, A_IN), BF16),
        compiler_params=pltpu.CompilerParams(dimension_semantics=("arbitrary",), vmem_limit_bytes=VMEM_LIMIT),
        name="in_proj_a",
    )(x2, g1, w, gq, gk, gm)


def _in_b_kernel(x_ref, g1_ref, w_ref, b_ref, gm_ref, h_ref, qm_ref):
    h = _rms(x_ref[...], g1_ref[...]).astype(BF16)
    n_ch = CONV_CH // COL_CHUNK
    for j in range(n_ch):
        ca = slice(j * COL_CHUNK, (j + 1) * COL_CHUNK)
        cg = slice(CONV_CH + j * COL_CHUNK, CONV_CH + (j + 1) * COL_CHUNK)
        a = jnp.dot(h, w_ref[:, ca], preferred_element_type=F32) + b_ref[:, ca]
        gate = jnp.dot(h, w_ref[:, cg], preferred_element_type=F32) + b_ref[:, cg]
        h_ref[:, ca] = a * jax.nn.sigmoid(gate)
    cm = slice(2 * CONV_CH, B_IN)
    zm = jnp.dot(h, w_ref[:, cm], preferred_element_type=F32) + b_ref[:, cm]
    qm_ref[...] = _head_norm(zm, gm_ref[...]).astype(BF16)


def _in_proj_b(x2, g1, w, b, gm):
    T = x2.shape[0]
    return pl.pallas_call(
        _in_b_kernel,
        grid=(T // TOK_TILE,),
        in_specs=[
            pl.BlockSpec((TOK_TILE, D_MODEL), lambda i: (i, 0)),
            _const_spec((1, D_MODEL)),
            _const_spec((D_MODEL, B_IN)),
            _const_spec((1, B_IN)),
            _const_spec((1, MEM_WIDTH)),
        ],
        out_specs=[
            pl.BlockSpec((TOK_TILE, CONV_CH), lambda i: (i, 0)),
            pl.BlockSpec((TOK_TILE, MEM_WIDTH), lambda i: (i, 0)),
        ],
        out_shape=[jax.ShapeDtypeStruct((T, CONV_CH), F32), jax.ShapeDtypeStruct((T, MEM_WIDTH), BF16)],
        compiler_params=pltpu.CompilerParams(dimension_semantics=("arbitrary",), vmem_limit_bytes=VMEM_LIMIT),
        name="in_proj_b",
    )(x2, g1, w, b, gm)


def _attn_kernel(q_ref, kp_ref, kc_ref, vp_ref, vc_ref, bias_ref, o_ref, kband, vband):
    first = pl.program_id(1) == 0
    kband[0:ATT_TILE] = kp_ref[...]
    kband[ATT_TILE:2 * ATT_TILE] = kc_ref[...]
    vband[0:ATT_TILE] = vp_ref[...]
    vband[ATT_TILE:2 * ATT_TILE] = vc_ref[...]
    col = lax.broadcasted_iota(jnp.int32, (PAIR, BAND), 1)
    for c in range(ATT_TILE // CHUNK):
        rows = slice(c * CHUNK, (c + 1) * CHUNK)
        band = slice(c * CHUNK, c * CHUNK + BAND)
        q2 = _stack_pair(q_ref[rows, :])
        s = _nt_dot(q2, kband[band, :]) + bias_ref[...]
        lim = jnp.where(first, BAND_PAD - c * CHUNK, 0)
        s = jnp.where(col >= lim, s, NEG_INF)
        m = jnp.max(s, axis=-1, keepdims=True)
        e = jnp.exp(s - m)
        l = jnp.sum(e, axis=-1, keepdims=True)
        o2 = jnp.dot(e.astype(BF16), vband[band, :], preferred_element_type=F32) * (1.0 / l)
        o_ref[rows, :] = _unstack_pair(o2).astype(BF16)


def _band_attention(z3, bias):
    B, S, _ = z3.shape
    n_pairs = TOK_WIDTH // LANES
    blk = (None, ATT_TILE, LANES)
    return pl.pallas_call(
        _attn_kernel,
        grid=(B, S // ATT_TILE, n_pairs),
        in_specs=[
            pl.BlockSpec(blk, lambda b, s, p: (b, s, p)),
            pl.BlockSpec(blk, lambda b, s, p: (b, jnp.maximum(s - 1, 0), n_pairs + p)),
            pl.BlockSpec(blk, lambda b, s, p: (b, s, n_pairs + p)),
            pl.BlockSpec(blk, lambda b, s, p: (b, jnp.maximum(s - 1, 0), 2 * n_pairs + p)),
            pl.BlockSpec(blk, lambda b, s, p: (b, s, 2 * n_pairs + p)),
            pl.BlockSpec((None, PAIR, BAND), lambda b, s, p: (p, 0, 0)),
        ],
        out_specs=pl.BlockSpec(blk, lambda b, s, p: (b, s, p)),
        out_shape=jax.ShapeDtypeStruct((B, S, TOK_WIDTH), BF16),
        scratch_shapes=[pltpu.VMEM((2 * ATT_TILE, LANES), BF16), pltpu.VMEM((2 * ATT_TILE, LANES), BF16)],
        compiler_params=pltpu.CompilerParams(dimension_semantics=("arbitrary",) * 3),
        name="band_attn",
    )(z3, z3, z3, z3, z3, bias)


def _conv_kernel(halo_ref, h_ref, w_ref, cb_ref, lg_ref, lb_ref, o_ref, hp_ref, y_ref):
    first = pl.program_id(1) == 0
    hp_ref[0:CONV_HALO, :] = jnp.where(first, 0.0, halo_ref[...])
    hp_ref[CONV_HALO:CONV_HALO + CONV_TILE, :] = h_ref[...]
    base = CONV_HALO - (CONV_WIDTH - 1)
    for cb in range(CONV_CH // LANES):
        cols = slice(cb * LANES, (cb + 1) * LANES)
        for r in range(0, CONV_TILE, CONV_ROWS):
            acc = jnp.zeros((CONV_ROWS, LANES), F32)
            for j in range(CONV_WIDTH):
                acc = acc + hp_ref[base + r + j:base + r + j + CONV_ROWS, cols] * w_ref[j:j + 1, cols]
            y_ref[r:r + CONV_ROWS, cols] = acc + cb_ref[:, cols]
    y = y_ref[...]
    mu = jnp.mean(y, axis=-1, keepdims=True)
    yc = y - mu
    yn = yc * lax.rsqrt(jnp.mean(yc * yc, axis=-1, keepdims=True) + EPS) * lg_ref[...] + lb_ref[...]
    o_ref[...] = (yn * jax.nn.sigmoid(yn)).astype(BF16)


def _conformer_conv(h3, w, cb, lg, lb):
    B, S, _ = h3.shape
    per = CONV_TILE // CONV_HALO
    return pl.pallas_call(
        _conv_kernel,
        grid=(B, S // CONV_TILE),
        in_specs=[
            pl.BlockSpec((None, CONV_HALO, CONV_CH), lambda b, s: (b, jnp.maximum(s * per - 1, 0), 0)),
            pl.BlockSpec((None, CONV_TILE, CONV_CH), lambda b, s: (b, s, 0)),
            _const_spec((CONV_WIDTH, CONV_CH)),
            _const_spec((1, CONV_CH)),
            _const_spec((1, CONV_CH)),
            _const_spec((1, CONV_CH)),
        ],
        out_specs=pl.BlockSpec((None, CONV_TILE, CONV_CH), lambda b, s: (b, s, 0)),
        out_shape=jax.ShapeDtypeStruct((B, S, CONV_CH), BF16),
        scratch_shapes=[pltpu.VMEM((CONV_HALO + CONV_TILE, CONV_CH), F32), pltpu.VMEM((CONV_TILE, CONV_CH), F32)],
        compiler_params=pltpu.CompilerParams(dimension_semantics=("arbitrary",) * 2),
        name="conformer_conv",
    )(h3, h3, w, cb, lg, lb)


def _post_kernel(x_ref, tok_ref, qm_ref, km_ref, vm_ref, wo_ref, g2_ref, wg_ref, wu_ref, wd_ref,
                 o_ref, memo_ref, h2_ref, acc_ref):
    for p in range(MEM_WIDTH // LANES):
        cols = slice(p * LANES, (p + 1) * LANES)
        q2 = _stack_pair(qm_ref[:, cols])
        s = _nt_dot(q2, km_ref[:, cols])
        m = jnp.max(s, axis=-1, keepdims=True)
        e = jnp.exp(s - m)
        l = jnp.sum(e, axis=-1, keepdims=True)
        o2 = jnp.dot(e.astype(BF16), vm_ref[:, cols], preferred_element_type=F32) * (1.0 / l)
        memo_ref[:, cols] = _unstack_pair(o2).astype(BF16)
    y = jnp.dot(tok_ref[...], wo_ref[0:TOK_WIDTH, :], preferred_element_type=F32)
    y = y + jnp.dot(memo_ref[...], wo_ref[TOK_WIDTH:D_MODEL, :], preferred_element_type=F32)
    x1 = x_ref[...] + y
    o_ref[...] = x1
    h2_ref[...] = _rms(x1, g2_ref[...]).astype(BF16)
    acc_ref[...] = jnp.zeros_like(acc_ref)

    def ff_step(c, carry):
        h2 = h2_ref[...]
        g = jnp.dot(h2, wg_ref[c], preferred_element_type=F32)
        u = jnp.dot(h2, wu_ref[c], preferred_element_type=F32)
        a = (g * jax.nn.sigmoid(g) * u).astype(BF16)
        acc_ref[...] += jnp.dot(a, wd_ref[c], preferred_element_type=F32)
        return carry

    lax.fori_loop(0, D_FF // FF_CHUNK, ff_step, 0)
    o_ref[...] += acc_ref[...]


def _post(x2, tok2, qm2, km, vm, wo, g2, wg, wu, wd, qm_block):
    T = x2.shape[0]
    tiles_per_seq = (T // km.shape[0]) // TOK_TILE
    n_ff = D_FF // FF_CHUNK
    return pl.pallas_call(
        _post_kernel,
        grid=(T // TOK_TILE,),
        in_specs=[
            pl.BlockSpec((TOK_TILE, D_MODEL), lambda i: (i, 0)),
            pl.BlockSpec((TOK_TILE, TOK_WIDTH), lambda i: (i, 0)),
            pl.BlockSpec((TOK_TILE, MEM_WIDTH), lambda i: (i, qm_block)),
            pl.BlockSpec((None, MEM_TOKENS, MEM_WIDTH), lambda i: (i // tiles_per_seq, 0, 0)),
            pl.BlockSpec((None, MEM_TOKENS, MEM_WIDTH), lambda i: (i // tiles_per_seq, 0, 0)),
            _const_spec((D_MODEL, D_MODEL)),
            _const_spec((1, D_MODEL)),
            _const_spec((n_ff, D_MODEL, FF_CHUNK)),
            _const_spec((n_ff, D_MODEL, FF_CHUNK)),
            _const_spec((n_ff, FF_CHUNK, D_MODEL)),
        ],
        out_specs=pl.BlockSpec((TOK_TILE, D_MODEL), lambda i: (i, 0)),
        out_shape=jax.ShapeDtypeStruct((T, D_MODEL), F32),
        scratch_shapes=[
            pltpu.VMEM((TOK_TILE, MEM_WIDTH), BF16),
            pltpu.VMEM((TOK_TILE, D_MODEL), BF16),
            pltpu.VMEM((TOK_TILE, D_MODEL), F32),
        ],
        compiler_params=pltpu.CompilerParams(dimension_semantics=("arbitrary",), vmem_limit_bytes=VMEM_LIMIT),
        name="post",
    )(x2, tok2, qm2, km, vm, wo, g2, wg, wu, wd)


def _row(v):
    return v.reshape(1, -1).astype(F32)


def _tile_heads(g, width, scale=1.0):
    return _row(jnp.tile(g * scale, width // HEAD_DIM))


def _ff_weights(w_gate, w_up, w_down):
    n_ff = D_FF // FF_CHUNK
    wg = w_gate.astype(BF16).reshape(D_MODEL, n_ff, FF_CHUNK).transpose(1, 0, 2)
    wu = w_up.astype(BF16).reshape(D_MODEL, n_ff, FF_CHUNK).transpose(1, 0, 2)
    wd = w_down.astype(BF16).reshape(n_ff, FF_CHUNK, D_MODEL)
    return wg, wu, wd


def kernel(x, mem, norm1_g, mem_norm_g, a_w_in, a_q_g, a_k_g, a_rel_bias, b_w_in, b_b_in, b_conv_w, b_conv_b,
           b_ln_g, b_ln_b, mq_g, mk_g, w_mem_kv, w_out, norm2_g, w_gate, w_up, w_down):
    B, S, D = x.shape
    T = B * S
    depth = norm1_g.shape[0]
    x2 = x.reshape(T, D)
    for i in range(depth):
        j = i // 2
        km, vm = _mem_kv(mem, _row(mem_norm_g[i]), w_mem_kv[i].astype(BF16), _tile_heads(mk_g[i], MEM_WIDTH))
        gm = _tile_heads(mq_g[i], MEM_WIDTH, ATTN_SCALE)
        if i % 2 == 0:
            z = _in_proj_a(x2, _row(norm1_g[i]), a_w_in[j].astype(BF16),
                           _tile_heads(a_q_g[j], COL_CHUNK, ATTN_SCALE), _tile_heads(a_k_g[j], COL_CHUNK), gm)
            bias = a_rel_bias[j][:, REL_IDX].astype(F32).reshape(ATT_HEADS // 2, PAIR, BAND)
            tok = _band_attention(z.reshape(B, S, A_IN), bias).reshape(T, TOK_WIDTH)
            qm2, qm_block = z, (3 * TOK_WIDTH) // MEM_WIDTH
        else:
            h, qm2 = _in_proj_b(x2, _row(norm1_g[i]), b_w_in[j].astype(BF16), _row(b_b_in[j]), gm)
            tok = _conformer_conv(h.reshape(B, S, CONV_CH), b_conv_w[j].astype(F32), _row(b_conv_b[j]),
                                  _row(b_ln_g[j]), _row(b_ln_b[j])).reshape(T, CONV_CH)
            qm_block = 0
        wg, wu, wd = _ff_weights(w_gate[i], w_up[i], w_down[i])
        x2 = _post(x2, tok, qm2, km, vm, w_out[i].astype(BF16), _row(norm2_g[i]), wg, wu, wd, qm_block)
    return x2.reshape(B, S, D)
```

```python
import jax
import jax.numpy as jnp
from jax import lax
from jax.experimental import pallas as pl
from jax.experimental.pallas import tpu as pltpu

D_MODEL = 1024
CHUNK = 64
HEAD_DIM = 64
MEM_TOKENS = 256
MEM_HEADS = 4
MEM_WIDTH = MEM_HEADS * HEAD_DIM
TOK_WIDTH = D_MODEL - MEM_WIDTH
ATT_HEADS = TOK_WIDTH // HEAD_DIM
LEFT_CHUNKS = 8
BAND = (LEFT_CHUNKS + 1) * CHUNK
BAND_PAD = LEFT_CHUNKS * CHUNK
REL_CLIP = 128
CONV_WIDTH = 31
CONV_CH = TOK_WIDTH
A_IN = 3 * TOK_WIDTH + MEM_WIDTH
B_IN = 2 * CONV_CH + MEM_WIDTH
D_FF = 2816
EPS = 1e-6
NEG_INF = -1e30
ATTN_SCALE = HEAD_DIM ** -0.5

LANES = 128
PAIR = 2 * HEAD_DIM
TOK_TILE = 512
ATT_BLK = 4 * CHUNK
ATT_KEYS = ATT_BLK + BAND_PAD
ATT_ROWS = 32
LOG2E = 1.4426950408889634
COL_CHUNK = 256
FF_CHUNK = 256
CONV_TILE = 512
CONV_HALO = 32
CONV_ROWS = 64
SUBLANES = 8
VMEM_LIMIT = 56 * 1024 * 1024

BF16 = jnp.bfloat16
F32 = jnp.float32


def _const_spec(shape):
    return pl.BlockSpec(shape, lambda *_: (0,) * len(shape), pipeline_mode=pl.Buffered(1))


def _rms(x, g):
    return x * lax.rsqrt(jnp.mean(x * x, axis=-1, keepdims=True) + EPS) * g


def _left_mask(shape):
    return lax.broadcasted_iota(jnp.int32, shape, len(shape) - 1) < HEAD_DIM


def _pair_head_norm(z, g):
    left = _left_mask(z.shape)
    sq = z * z
    ss_l = jnp.sum(jnp.where(left, sq, 0.0), axis=-1, keepdims=True)
    ss_r = jnp.sum(jnp.where(left, 0.0, sq), axis=-1, keepdims=True)
    r = jnp.where(left, lax.rsqrt(ss_l * (1.0 / HEAD_DIM) + EPS), lax.rsqrt(ss_r * (1.0 / HEAD_DIM) + EPS))
    return z * r * g


def _head_norm(z, g):
    parts = [_pair_head_norm(z[:, p * LANES:(p + 1) * LANES], g[:, p * LANES:(p + 1) * LANES])
             for p in range(z.shape[1] // LANES)]
    return jnp.concatenate(parts, axis=1) if len(parts) > 1 else parts[0]


def _stack_pair(q):
    left = _left_mask(q.shape)
    zero = jnp.zeros_like(q)
    return jnp.concatenate([jnp.where(left, q, zero), jnp.where(left, zero, q)], axis=0)


def _unstack_pair(o2):
    rows = o2.shape[0] // 2
    left = _left_mask((rows, LANES))
    return jnp.where(left, o2[:rows], o2[rows:])


def _nt_dot(a, b):
    return lax.dot_general(a, b, (((1,), (1,)), ((), ())), preferred_element_type=F32)


def _mem_kv_kernel(mem_ref, g_ref, w_ref, gk_ref, k_ref, v_ref):
    m = _rms(mem_ref[...], g_ref[...]).astype(BF16)
    kv = jnp.dot(m, w_ref[...], preferred_element_type=F32)
    k_ref[...] = _head_norm(kv[:, :MEM_WIDTH], gk_ref[...]).astype(BF16)
    v_ref[...] = kv[:, MEM_WIDTH:].astype(BF16)


def _mem_kv(mem, g, w, gk):
    B = mem.shape[0]
    return pl.pallas_call(
        _mem_kv_kernel,
        grid=(B,),
        in_specs=[
            pl.BlockSpec((None, MEM_TOKENS, D_MODEL), lambda b: (b, 0, 0)),
            _const_spec((1, D_MODEL)),
            _const_spec((D_MODEL, 2 * MEM_WIDTH)),
            _const_spec((1, MEM_WIDTH)),
        ],
        out_specs=[
            pl.BlockSpec((None, MEM_TOKENS, MEM_WIDTH), lambda b: (b, 0, 0)),
            pl.BlockSpec((None, MEM_TOKENS, MEM_WIDTH), lambda b: (b, 0, 0)),
        ],
        out_shape=[jax.ShapeDtypeStruct((B, MEM_TOKENS, MEM_WIDTH), BF16)] * 2,
        compiler_params=pltpu.CompilerParams(dimension_semantics=("arbitrary",)),
        name="mem_kv",
    )(mem, g, w, gk)


def _in_a_kernel(x_ref, g1_ref, w_ref, gq_ref, gk_ref, gm_ref, o_ref):
    h = _rms(x_ref[...], g1_ref[...]).astype(BF16)
    n_tok = TOK_WIDTH // COL_CHUNK
    for j in range(A_IN // COL_CHUNK):
        cols = slice(j * COL_CHUNK, (j + 1) * COL_CHUNK)
        z = jnp.dot(h, w_ref[:, cols], preferred_element_type=F32)
        if j < n_tok:
            z = _head_norm(z, gq_ref[...])
        elif j < 2 * n_tok:
            z = _head_norm(z, gk_ref[...])
        elif j >= 3 * n_tok:
            z = _head_norm(z, gm_ref[...])
        o_ref[:, cols] = z.astype(BF16)


def _in_proj_a(x2, g1, w, gq, gk, gm):
    T = x2.shape[0]
    return pl.pallas_call(
        _in_a_kernel,
        grid=(T // TOK_TILE,),
        in_specs=[
            pl.BlockSpec((TOK_TILE, D_MODEL), lambda i: (i, 0)),
            _const_spec((1, D_MODEL)),
            _const_spec((D_MODEL, A_IN)),
            _const_spec((1, COL_CHUNK)),
            _const_spec((1, COL_CHUNK)),
            _const_spec((1, COL_CHUNK)),
        ],
        out_specs=pl.BlockSpec((TOK_TILE, A_IN), lambda i: (i, 0)),
        out_shape=jax.ShapeDtypeStruct((T, A_IN), BF16),
        compiler_params=pltpu.CompilerParams(dimension_semantics=("arbitrary",), vmem_limit_bytes=VMEM_LIMIT),
        name="in_proj_a",
    )(x2, g1, w, gq, gk, gm)


def _in_b_kernel(x_ref, g1_ref, w_ref, b_ref, gm_ref, h_ref, qm_ref):
    h = _rms(x_ref[...], g1_ref[...]).astype(BF16)
    n_ch = CONV_CH // COL_CHUNK
    for j in range(n_ch):
        ca = slice(j * COL_CHUNK, (j + 1) * COL_CHUNK)
        cg = slice(CONV_CH + j * COL_CHUNK, CONV_CH + (j + 1) * COL_CHUNK)
        a = jnp.dot(h, w_ref[:, ca], preferred_element_type=F32) + b_ref[:, ca]
        gate = jnp.dot(h, w_ref[:, cg], preferred_element_type=F32) + b_ref[:, cg]
        h_ref[:, ca] = a * jax.nn.sigmoid(gate)
    cm = slice(2 * CONV_CH, B_IN)
    zm = jnp.dot(h, w_ref[:, cm], preferred_element_type=F32) + b_ref[:, cm]
    qm_ref[...] = _head_norm(zm, gm_ref[...]).astype(BF16)


def _in_proj_b(x2, g1, w, b, gm):
    T = x2.shape[0]
    return pl.pallas_call(
        _in_b_kernel,
        grid=(T // TOK_TILE,),
        in_specs=[
            pl.BlockSpec((TOK_TILE, D_MODEL), lambda i: (i, 0)),
            _const_spec((1, D_MODEL)),
            _const_spec((D_MODEL, B_IN)),
            _const_spec((1, B_IN)),
            _const_spec((1, MEM_WIDTH)),
        ],
        out_specs=[
            pl.BlockSpec((TOK_TILE, CONV_CH), lambda i: (i, 0)),
            pl.BlockSpec((TOK_TILE, MEM_WIDTH), lambda i: (i, 0)),
        ],
        out_shape=[jax.ShapeDtypeStruct((T, CONV_CH), F32), jax.ShapeDtypeStruct((T, MEM_WIDTH), BF16)],
        compiler_params=pltpu.CompilerParams(dimension_semantics=("arbitrary",), vmem_limit_bytes=VMEM_LIMIT),
        name="in_proj_b",
    )(x2, g1, w, b, gm)


def _attn_kernel(q_ref, k_ref, v_ref, bias_ref, o_ref, s_ref, p_ref, il_ref):
    seq = q_ref.shape[0]
    for i in range(seq // ATT_BLK):
        q0 = i * ATT_BLK
        k_lo = max(q0 - BAND_PAD, 0)
        k_hi = q0 + ATT_BLK
        n_keys = k_hi - k_lo
        b_lo = ATT_KEYS - n_keys
        q2 = _stack_pair(q_ref[q0:q0 + ATT_BLK, :])
        s_ref[:, 0:n_keys] = _nt_dot(q2, k_ref[k_lo:k_hi, :])
        for r in range(0, 2 * ATT_BLK, ATT_ROWS):
            s = s_ref[r:r + ATT_ROWS, 0:n_keys] + bias_ref[r:r + ATT_ROWS, b_lo:ATT_KEYS]
            m = jnp.max(s, axis=-1, keepdims=True)
            e = jnp.exp2(s - m)
            l = jnp.sum(e, axis=-1, keepdims=True)
            p_ref[r:r + ATT_ROWS, 0:n_keys] = e.astype(BF16)
            il_ref[r:r + ATT_ROWS, :] = jnp.broadcast_to(1.0 / l, (ATT_ROWS, LANES))
        o2 = jnp.dot(p_ref[:, 0:n_keys], v_ref[k_lo:k_hi, :], preferred_element_type=F32) * il_ref[...]
        o_ref[q0:q0 + ATT_BLK, :] = _unstack_pair(o2).astype(BF16)


def _band_attention(z3, bias):
    B, S, _ = z3.shape
    n_pairs = TOK_WIDTH // LANES
    blk = (None, S, LANES)
    return pl.pallas_call(
        _attn_kernel,
        grid=(n_pairs, B),
        in_specs=[
            pl.BlockSpec(blk, lambda p, b: (b, 0, p)),
            pl.BlockSpec(blk, lambda p, b: (b, 0, n_pairs + p)),
            pl.BlockSpec(blk, lambda p, b: (b, 0, 2 * n_pairs + p)),
            pl.BlockSpec((None, 2 * ATT_BLK, ATT_KEYS), lambda p, b: (p, 0, 0)),
        ],
        out_specs=pl.BlockSpec(blk, lambda p, b: (b, 0, p)),
        out_shape=jax.ShapeDtypeStruct((B, S, TOK_WIDTH), BF16),
        scratch_shapes=[
            pltpu.VMEM((2 * ATT_BLK, ATT_KEYS), F32),
            pltpu.VMEM((2 * ATT_BLK, ATT_KEYS), BF16),
            pltpu.VMEM((2 * ATT_BLK, LANES), F32),
        ],
        compiler_params=pltpu.CompilerParams(dimension_semantics=("arbitrary",) * 2),
        name="band_attn",
    )(z3, z3, z3, bias)


def _band_bias(rel_bias):
    n_rel = rel_bias.shape[1]
    ext = jnp.concatenate(
        [jnp.broadcast_to(rel_bias[:, n_rel - 1:], (ATT_HEADS, BAND - REL_CLIP)), rel_bias[:, n_rel - 2::-1]], axis=1)
    chunk = jnp.stack([ext[:, CHUNK - 1 - qi:CHUNK - 1 - qi + BAND] for qi in range(CHUNK)], axis=1)
    n_cq = ATT_BLK // CHUNK
    blk = jnp.stack(
        [jnp.pad(chunk, ((0, 0), (0, 0), (cq * CHUNK, ATT_KEYS - BAND - cq * CHUNK)), constant_values=NEG_INF)
         for cq in range(n_cq)], axis=1)
    return (blk * LOG2E).reshape(ATT_HEADS // 2, 2 * ATT_BLK, ATT_KEYS).astype(F32)


def _conv_kernel(halo_ref, h_ref, w_ref, cb_ref, lg_ref, lb_ref, o_ref, hs_ref, y_ref):
    first = pl.program_id(1) == 0
    hs_ref[0, 0:CONV_HALO, :] = jnp.where(first, 0.0, halo_ref[...])
    hs_ref[0, CONV_HALO:CONV_HALO + CONV_TILE, :] = h_ref[...]
    n_shift = hs_ref.shape[1] - SUBLANES
    for b in range(1, SUBLANES):
        hs_ref[b, 0:n_shift, :] = hs_ref[0, b:b + n_shift, :]
    base = CONV_HALO - (CONV_WIDTH - 1)

    n_blocks = CONV_TILE // CONV_ROWS

    def conv_rows(rb):
        r = rb * CONV_ROWS if isinstance(rb, int) else pl.multiple_of(rb * CONV_ROWS, CONV_ROWS)
        for cb in range(CONV_CH // LANES):
            cols = slice(cb * LANES, (cb + 1) * LANES)
            acc = [jnp.zeros((SUBLANES, LANES), F32)] * (CONV_ROWS // SUBLANES)
            for j in range(CONV_WIDTH):
                b, a = (base + j) % SUBLANES, (base + j) // SUBLANES * SUBLANES
                wj = w_ref[j, :, cols]
                acc = [acc[k] + hs_ref[b, pl.ds(r + a + k * SUBLANES, SUBLANES), cols] * wj
                       for k in range(CONV_ROWS // SUBLANES)]
            for k in range(CONV_ROWS // SUBLANES):
                y_ref[rb % 2, k * SUBLANES:(k + 1) * SUBLANES, cols] = acc[k] + cb_ref[:, cols]

    def norm_rows(rb):
        r = rb * CONV_ROWS if isinstance(rb, int) else pl.multiple_of(rb * CONV_ROWS, CONV_ROWS)
        y = y_ref[rb % 2]
        mu = jnp.mean(y, axis=-1, keepdims=True)
        yc = y - mu
        yn = yc * lax.rsqrt(jnp.mean(yc * yc, axis=-1, keepdims=True) + EPS) * lg_ref[...] + lb_ref[...]
        o_ref[pl.ds(r, CONV_ROWS), :] = (yn * jax.nn.sigmoid(yn)).astype(BF16)

    def step(rb, carry):
        norm_rows(rb - 1)
        conv_rows(rb)
        return carry

    conv_rows(0)
    lax.fori_loop(1, n_blocks, step, 0)
    norm_rows(n_blocks - 1)


def _conformer_conv(h3, w, cb, lg, lb):
    B, S, _ = h3.shape
    per = CONV_TILE // CONV_HALO
    return pl.pallas_call(
        _conv_kernel,
        grid=(B, S // CONV_TILE),
        in_specs=[
            pl.BlockSpec((None, CONV_HALO, CONV_CH), lambda b, s: (b, jnp.maximum(s * per - 1, 0), 0)),
            pl.BlockSpec((None, CONV_TILE, CONV_CH), lambda b, s: (b, s, 0)),
            _const_spec((CONV_WIDTH, SUBLANES, CONV_CH)),
            _const_spec((1, CONV_CH)),
            _const_spec((1, CONV_CH)),
            _const_spec((1, CONV_CH)),
        ],
        out_specs=pl.BlockSpec((None, CONV_TILE, CONV_CH), lambda b, s: (b, s, 0)),
        out_shape=jax.ShapeDtypeStruct((B, S, CONV_CH), BF16),
        scratch_shapes=[pltpu.VMEM((SUBLANES, CONV_HALO + CONV_TILE, CONV_CH), F32),
                        pltpu.VMEM((2, CONV_ROWS, CONV_CH), F32)],
        compiler_params=pltpu.CompilerParams(dimension_semantics=("arbitrary",) * 2, vmem_limit_bytes=VMEM_LIMIT),
        name="conformer_conv",
    )(h3, h3, w, cb, lg, lb)


def _post_kernel(x_ref, tok_ref, qm_ref, km_ref, vm_ref, wo_ref, g2_ref, wg_ref, wu_ref, wd_ref,
                 o_ref, memo_ref, h2_ref, acc_ref):
    for p in range(MEM_WIDTH // LANES):
        cols = slice(p * LANES, (p + 1) * LANES)
        q2 = _stack_pair(qm_ref[:, cols])
        s = _nt_dot(q2, km_ref[:, cols])
        m = jnp.max(s, axis=-1, keepdims=True)
        e = jnp.exp2(s - m)
        l = jnp.sum(e, axis=-1, keepdims=True)
        o2 = jnp.dot(e.astype(BF16), vm_ref[:, cols], preferred_element_type=F32) * (1.0 / l)
        memo_ref[:, cols] = _unstack_pair(o2).astype(BF16)
    y = jnp.dot(tok_ref[...], wo_ref[0:TOK_WIDTH, :], preferred_element_type=F32)
    y = y + jnp.dot(memo_ref[...], wo_ref[TOK_WIDTH:D_MODEL, :], preferred_element_type=F32)
    x1 = x_ref[...] + y
    o_ref[...] = x1
    h2_ref[...] = _rms(x1, g2_ref[...]).astype(BF16)
    acc_ref[...] = jnp.zeros_like(acc_ref)

    def ff_step(c, carry):
        h2 = h2_ref[...]
        g = jnp.dot(h2, wg_ref[c], preferred_element_type=F32)
        u = jnp.dot(h2, wu_ref[c], preferred_element_type=F32)
        a = (g * jax.nn.sigmoid(g) * u).astype(BF16)
        acc_ref[...] += jnp.dot(a, wd_ref[c], preferred_element_type=F32)
        return carry

    lax.fori_loop(0, D_FF // FF_CHUNK, ff_step, 0)
    o_ref[...] += acc_ref[...]


def _post(x2, tok2, qm2, km, vm, wo, g2, wg, wu, wd, qm_block):
    T = x2.shape[0]
    tiles_per_seq = (T // km.shape[0]) // TOK_TILE
    n_ff = D_FF // FF_CHUNK
    return pl.pallas_call(
        _post_kernel,
        grid=(T // TOK_TILE,),
        in_specs=[
            pl.BlockSpec((TOK_TILE, D_MODEL), lambda i: (i, 0)),
            pl.BlockSpec((TOK_TILE, TOK_WIDTH), lambda i: (i, 0)),
            pl.BlockSpec((TOK_TILE, MEM_WIDTH), lambda i: (i, qm_block)),
            pl.BlockSpec((None, MEM_TOKENS, MEM_WIDTH), lambda i: (i // tiles_per_seq, 0, 0)),
            pl.BlockSpec((None, MEM_TOKENS, MEM_WIDTH), lambda i: (i // tiles_per_seq, 0, 0)),
            _const_spec((D_MODEL, D_MODEL)),
            _const_spec((1, D_MODEL)),
            _const_spec((n_ff, D_MODEL, FF_CHUNK)),
            _const_spec((n_ff, D_MODEL, FF_CHUNK)),
            _const_spec((n_ff, FF_CHUNK, D_MODEL)),
        ],
        out_specs=pl.BlockSpec((TOK_TILE, D_MODEL), lambda i: (i, 0)),
        out_shape=jax.ShapeDtypeStruct((T, D_MODEL), F32),
        scratch_shapes=[
            pltpu.VMEM((TOK_TILE, MEM_WIDTH), BF16),
            pltpu.VMEM((TOK_TILE, D_MODEL), BF16),
            pltpu.VMEM((TOK_TILE, D_MODEL), F32),
        ],
        compiler_params=pltpu.CompilerParams(dimension_semantics=("arbitrary",), vmem_limit_bytes=VMEM_LIMIT),
        name="post",
    )(x2, tok2, qm2, km, vm, wo, g2, wg, wu, wd)


def _row(v):
    return v.reshape(1, -1).astype(F32)


def _tile_heads(g, width, scale=1.0):
    return _row(jnp.tile(g * scale, width // HEAD_DIM))


def _ff_weights(w_gate, w_up, w_down):
    n_ff = D_FF // FF_CHUNK
    wg = w_gate.astype(BF16).reshape(D_MODEL, n_ff, FF_CHUNK).transpose(1, 0, 2)
    wu = w_up.astype(BF16).reshape(D_MODEL, n_ff, FF_CHUNK).transpose(1, 0, 2)
    wd = w_down.astype(BF16).reshape(n_ff, FF_CHUNK, D_MODEL)
    return wg, wu, wd


def kernel(x, mem, norm1_g, mem_norm_g, a_w_in, a_q_g, a_k_g, a_rel_bias, b_w_in, b_b_in, b_conv_w, b_conv_b,
           b_ln_g, b_ln_b, mq_g, mk_g, w_mem_kv, w_out, norm2_g, w_gate, w_up, w_down):
    B, S, D = x.shape
    T = B * S
    depth = norm1_g.shape[0]
    x2 = x.reshape(T, D)
    for i in range(depth):
        j = i // 2
        km, vm = _mem_kv(mem, _row(mem_norm_g[i]), w_mem_kv[i].astype(BF16), _tile_heads(mk_g[i], MEM_WIDTH))
        gm = _tile_heads(mq_g[i], MEM_WIDTH, ATTN_SCALE * LOG2E)
        if i % 2 == 0:
            z = _in_proj_a(x2, _row(norm1_g[i]), a_w_in[j].astype(BF16),
                           _tile_heads(a_q_g[j], COL_CHUNK, ATTN_SCALE * LOG2E), _tile_heads(a_k_g[j], COL_CHUNK), gm)
            tok = _band_attention(z.reshape(B, S, A_IN), _band_bias(a_rel_bias[j])).reshape(T, TOK_WIDTH)
            qm2, qm_block = z, (3 * TOK_WIDTH) // MEM_WIDTH
        else:
            h, qm2 = _in_proj_b(x2, _row(norm1_g[i]), b_w_in[j].astype(BF16), _row(b_b_in[j]), gm)
            conv_w = jnp.broadcast_to(b_conv_w[j].astype(F32)[:, None, :], (CONV_WIDTH, SUBLANES, CONV_CH))
            tok = _conformer_conv(h.reshape(B, S, CONV_CH), conv_w, _row(b_conv_b[j]),
                                  _row(b_ln_g[j]), _row(b_ln_b[j])).reshape(T, CONV_CH)
            qm_block = 0
        wg, wu, wd = _ff_weights(w_gate[i], w_up[i], w_down[i])
        x2 = _post(x2, tok, qm2, km, vm, w_out[i].astype(BF16), _row(norm2_g[i]), wg, wu, wd, qm_block)
    return x2.reshape(B, S, D)
```

```python
import jax
import jax.numpy as jnp
from jax import lax
from jax.experimental import pallas as pl
from jax.experimental.pallas import tpu as pltpu

D_MODEL = 1024
CHUNK = 64
HEAD_DIM = 64
MEM_TOKENS = 256
MEM_HEADS = 4
MEM_WIDTH = MEM_HEADS * HEAD_DIM
TOK_WIDTH = D_MODEL - MEM_WIDTH
ATT_HEADS = TOK_WIDTH // HEAD_DIM
LEFT_CHUNKS = 8
BAND = (LEFT_CHUNKS + 1) * CHUNK
BAND_PAD = LEFT_CHUNKS * CHUNK
REL_CLIP = 128
CONV_WIDTH = 31
CONV_CH = TOK_WIDTH
A_IN = 3 * TOK_WIDTH + MEM_WIDTH
B_IN = 2 * CONV_CH + MEM_WIDTH
D_FF = 2816
EPS = 1e-6
NEG_INF = -1e30
ATTN_SCALE = HEAD_DIM ** -0.5

LANES = 128
PAIR = 2 * HEAD_DIM
TOK_TILE = 512
ATT_BLK = 4 * CHUNK
ATT_KEYS = ATT_BLK + BAND_PAD
ATT_ROWS = 32
LOG2E = 1.4426950408889634
COL_CHUNK = 256
FF_CHUNK = 256
CONV_HALO = 32
CONV_TILE = 512
CONV_ROWS = 64
SUBLANES = 8
VMEM_LIMIT = 56 * 1024 * 1024

BF16 = jnp.bfloat16
F32 = jnp.float32


def _const_spec(shape):
    return pl.BlockSpec(shape, lambda *_: (0,) * len(shape), pipeline_mode=pl.Buffered(1))


def _rms(x, g):
    return x * lax.rsqrt(jnp.mean(x * x, axis=-1, keepdims=True) + EPS) * g


def _left_mask(shape):
    return lax.broadcasted_iota(jnp.int32, shape, len(shape) - 1) < HEAD_DIM


def _pair_head_norm(z, g):
    left = _left_mask(z.shape)
    sq = z * z
    ss_l = jnp.sum(jnp.where(left, sq, 0.0), axis=-1, keepdims=True)
    ss_r = jnp.sum(jnp.where(left, 0.0, sq), axis=-1, keepdims=True)
    r = jnp.where(left, lax.rsqrt(ss_l * (1.0 / HEAD_DIM) + EPS), lax.rsqrt(ss_r * (1.0 / HEAD_DIM) + EPS))
    return z * r * g


def _head_norm(z, g):
    parts = [_pair_head_norm(z[:, p * LANES:(p + 1) * LANES], g[:, p * LANES:(p + 1) * LANES])
             for p in range(z.shape[1] // LANES)]
    return jnp.concatenate(parts, axis=1) if len(parts) > 1 else parts[0]


def _stack_pair(q):
    left = _left_mask(q.shape)
    zero = jnp.zeros_like(q)
    return jnp.concatenate([jnp.where(left, q, zero), jnp.where(left, zero, q)], axis=0)


def _unstack_pair(o2):
    rows = o2.shape[0] // 2
    left = _left_mask((rows, LANES))
    return jnp.where(left, o2[:rows], o2[rows:])


def _nt_dot(a, b):
    return lax.dot_general(a, b, (((1,), (1,)), ((), ())), preferred_element_type=F32)


def _mem_kv_kernel(mem_ref, g_ref, w_ref, gk_ref, k_ref, v_ref):
    m = _rms(mem_ref[...], g_ref[...]).astype(BF16)
    kv = jnp.dot(m, w_ref[...], preferred_element_type=F32)
    k_ref[...] = _head_norm(kv[:, :MEM_WIDTH], gk_ref[...]).astype(BF16)
    v_ref[...] = kv[:, MEM_WIDTH:].astype(BF16)


def _mem_kv(mem, g, w, gk):
    B = mem.shape[0]
    return pl.pallas_call(
        _mem_kv_kernel,
        grid=(B,),
        in_specs=[
            pl.BlockSpec((None, MEM_TOKENS, D_MODEL), lambda b: (b, 0, 0)),
            _const_spec((1, D_MODEL)),
            _const_spec((D_MODEL, 2 * MEM_WIDTH)),
            _const_spec((1, MEM_WIDTH)),
        ],
        out_specs=[
            pl.BlockSpec((None, MEM_TOKENS, MEM_WIDTH), lambda b: (b, 0, 0)),
            pl.BlockSpec((None, MEM_TOKENS, MEM_WIDTH), lambda b: (b, 0, 0)),
        ],
        out_shape=[jax.ShapeDtypeStruct((B, MEM_TOKENS, MEM_WIDTH), BF16)] * 2,
        compiler_params=pltpu.CompilerParams(dimension_semantics=("arbitrary",)),
        name="mem_kv",
    )(mem, g, w, gk)


def _in_a_kernel(x_ref, g1_ref, w_ref, gq_ref, gk_ref, gm_ref, o_ref):
    h = _rms(x_ref[...], g1_ref[...]).astype(BF16)
    n_tok = TOK_WIDTH // COL_CHUNK
    for j in range(A_IN // COL_CHUNK):
        cols = slice(j * COL_CHUNK, (j + 1) * COL_CHUNK)
        z = jnp.dot(h, w_ref[:, cols], preferred_element_type=F32)
        if j < n_tok:
            z = _head_norm(z, gq_ref[...])
        elif j < 2 * n_tok:
            z = _head_norm(z, gk_ref[...])
        elif j >= 3 * n_tok:
            z = _head_norm(z, gm_ref[...])
        o_ref[:, cols] = z.astype(BF16)


def _in_proj_a(x2, g1, w, gq, gk, gm):
    T = x2.shape[0]
    return pl.pallas_call(
        _in_a_kernel,
        grid=(T // TOK_TILE,),
        in_specs=[
            pl.BlockSpec((TOK_TILE, D_MODEL), lambda i: (i, 0)),
            _const_spec((1, D_MODEL)),
            _const_spec((D_MODEL, A_IN)),
            _const_spec((1, COL_CHUNK)),
            _const_spec((1, COL_CHUNK)),
            _const_spec((1, COL_CHUNK)),
        ],
        out_specs=pl.BlockSpec((TOK_TILE, A_IN), lambda i: (i, 0)),
        out_shape=jax.ShapeDtypeStruct((T, A_IN), BF16),
        compiler_params=pltpu.CompilerParams(dimension_semantics=("arbitrary",), vmem_limit_bytes=VMEM_LIMIT),
        name="in_proj_a",
    )(x2, g1, w, gq, gk, gm)


def _in_b_kernel(x_ref, g1_ref, w_ref, b_ref, gm_ref, h_ref, qm_ref):
    h = _rms(x_ref[...], g1_ref[...]).astype(BF16)
    n_ch = CONV_CH // COL_CHUNK
    for j in range(n_ch):
        ca = slice(j * COL_CHUNK, (j + 1) * COL_CHUNK)
        cg = slice(CONV_CH + j * COL_CHUNK, CONV_CH + (j + 1) * COL_CHUNK)
        a = jnp.dot(h, w_ref[:, ca], preferred_element_type=F32) + b_ref[:, ca]
        gate = jnp.dot(h, w_ref[:, cg], preferred_element_type=F32) + b_ref[:, cg]
        h_ref[:, ca] = a * jax.nn.sigmoid(gate)
    cm = slice(2 * CONV_CH, B_IN)
    zm = jnp.dot(h, w_ref[:, cm], preferred_element_type=F32) + b_ref[:, cm]
    qm_ref[...] = _head_norm(zm, gm_ref[...]).astype(BF16)


def _in_proj_b(x2, g1, w, b, gm):
    T = x2.shape[0]
    return pl.pallas_call(
        _in_b_kernel,
        grid=(T // TOK_TILE,),
        in_specs=[
            pl.BlockSpec((TOK_TILE, D_MODEL), lambda i: (i, 0)),
            _const_spec((1, D_MODEL)),
            _const_spec((D_MODEL, B_IN)),
            _const_spec((1, B_IN)),
            _const_spec((1, MEM_WIDTH)),
        ],
        out_specs=[
            pl.BlockSpec((TOK_TILE, CONV_CH), lambda i: (i, 0)),
            pl.BlockSpec((TOK_TILE, MEM_WIDTH), lambda i: (i, 0)),
        ],
        out_shape=[jax.ShapeDtypeStruct((T, CONV_CH), F32), jax.ShapeDtypeStruct((T, MEM_WIDTH), BF16)],
        compiler_params=pltpu.CompilerParams(dimension_semantics=("arbitrary",), vmem_limit_bytes=VMEM_LIMIT),
        name="in_proj_b",
    )(x2, g1, w, b, gm)


def _attn_kernel(q_ref, k_ref, v_ref, bias_ref, o_ref, s_ref, p_ref, il_ref):
    seq = q_ref.shape[0]
    for i in range(seq // ATT_BLK):
        q0 = i * ATT_BLK
        k_lo = max(q0 - BAND_PAD, 0)
        k_hi = q0 + ATT_BLK
        n_keys = k_hi - k_lo
        b_lo = ATT_KEYS - n_keys
        q2 = _stack_pair(q_ref[q0:q0 + ATT_BLK, :])
        s_ref[:, 0:n_keys] = _nt_dot(q2, k_ref[k_lo:k_hi, :])
        for r in range(0, 2 * ATT_BLK, ATT_ROWS):
            s = s_ref[r:r + ATT_ROWS, 0:n_keys] + bias_ref[r:r + ATT_ROWS, b_lo:ATT_KEYS]
            m = jnp.max(s, axis=-1, keepdims=True)
            e = jnp.exp2(s - m)
            l = jnp.sum(e, axis=-1, keepdims=True)
            p_ref[r:r + ATT_ROWS, 0:n_keys] = e.astype(BF16)
            il_ref[r:r + ATT_ROWS, :] = jnp.broadcast_to(1.0 / l, (ATT_ROWS, LANES))
        o2 = jnp.dot(p_ref[:, 0:n_keys], v_ref[k_lo:k_hi, :], preferred_element_type=F32) * il_ref[...]
        o_ref[q0:q0 + ATT_BLK, :] = _unstack_pair(o2).astype(BF16)


def _band_attention(z3, bias):
    B, S, _ = z3.shape
    n_pairs = TOK_WIDTH // LANES
    blk = (None, S, LANES)
    return pl.pallas_call(
        _attn_kernel,
        grid=(n_pairs, B),
        in_specs=[
            pl.BlockSpec(blk, lambda p, b: (b, 0, p)),
            pl.BlockSpec(blk, lambda p, b: (b, 0, n_pairs + p)),
            pl.BlockSpec(blk, lambda p, b: (b, 0, 2 * n_pairs + p)),
            pl.BlockSpec((None, 2 * ATT_BLK, ATT_KEYS), lambda p, b: (p, 0, 0)),
        ],
        out_specs=pl.BlockSpec(blk, lambda p, b: (b, 0, p)),
        out_shape=jax.ShapeDtypeStruct((B, S, TOK_WIDTH), BF16),
        scratch_shapes=[
            pltpu.VMEM((2 * ATT_BLK, ATT_KEYS), F32),
            pltpu.VMEM((2 * ATT_BLK, ATT_KEYS), BF16),
            pltpu.VMEM((2 * ATT_BLK, LANES), F32),
        ],
        compiler_params=pltpu.CompilerParams(dimension_semantics=("arbitrary",) * 2),
        name="band_attn",
    )(z3, z3, z3, bias)


def _band_bias(rel_bias):
    n_rel = rel_bias.shape[1]
    ext = jnp.concatenate(
        [jnp.broadcast_to(rel_bias[:, n_rel - 1:], (ATT_HEADS, BAND - REL_CLIP)), rel_bias[:, n_rel - 2::-1]], axis=1)
    chunk = jnp.stack([ext[:, CHUNK - 1 - qi:CHUNK - 1 - qi + BAND] for qi in range(CHUNK)], axis=1)
    n_cq = ATT_BLK // CHUNK
    blk = jnp.stack(
        [jnp.pad(chunk, ((0, 0), (0, 0), (cq * CHUNK, ATT_KEYS - BAND - cq * CHUNK)), constant_values=NEG_INF)
         for cq in range(n_cq)], axis=1)
    return (blk * LOG2E).reshape(ATT_HEADS // 2, 2 * ATT_BLK, ATT_KEYS).astype(F32)


def _conv_kernel(halo_ref, h_ref, w_ref, cb_ref, lg_ref, lb_ref, o_ref, hs_ref, y_ref):
    first = pl.program_id(1) == 0
    hs_ref[0, 0:CONV_HALO, :] = jnp.where(first, 0.0, halo_ref[...])
    hs_ref[0, CONV_HALO:CONV_HALO + CONV_TILE, :] = h_ref[...]
    n_shift = hs_ref.shape[1] - SUBLANES
    for b in range(1, SUBLANES):
        hs_ref[b, 0:n_shift, :] = hs_ref[0, b:b + n_shift, :]
    base = CONV_HALO - (CONV_WIDTH - 1)

    n_blocks = CONV_TILE // CONV_ROWS

    def conv_rows(rb):
        r = rb * CONV_ROWS if isinstance(rb, int) else pl.multiple_of(rb * CONV_ROWS, CONV_ROWS)
        for cb in range(CONV_CH // LANES):
            cols = slice(cb * LANES, (cb + 1) * LANES)
            acc = [jnp.zeros((SUBLANES, LANES), F32)] * (CONV_ROWS // SUBLANES)
            for j in range(CONV_WIDTH):
                b, a = (base + j) % SUBLANES, (base + j) // SUBLANES * SUBLANES
                wj = w_ref[j, :, cols]
                acc = [acc[k] + hs_ref[b, pl.ds(r + a + k * SUBLANES, SUBLANES), cols] * wj
                       for k in range(CONV_ROWS // SUBLANES)]
            for k in range(CONV_ROWS // SUBLANES):
                y_ref[rb % 2, k * SUBLANES:(k + 1) * SUBLANES, cols] = acc[k] + cb_ref[:, cols]

    def norm_rows(rb):
        r = rb * CONV_ROWS if isinstance(rb, int) else pl.multiple_of(rb * CONV_ROWS, CONV_ROWS)
        y = y_ref[rb % 2]
        mu = jnp.mean(y, axis=-1, keepdims=True)
        yc = y - mu
        yn = yc * lax.rsqrt(jnp.mean(yc * yc, axis=-1, keepdims=True) + EPS) * lg_ref[...] + lb_ref[...]
        o_ref[pl.ds(r, CONV_ROWS), :] = (yn * jax.nn.sigmoid(yn)).astype(BF16)

    def step(rb, carry):
        norm_rows(rb - 1)
        conv_rows(rb)
        return carry

    conv_rows(0)
    lax.fori_loop(1, n_blocks, step, 0)
    norm_rows(n_blocks - 1)


def _conformer_conv(h3, w, cb, lg, lb):
    B, S, _ = h3.shape
    per = CONV_TILE // CONV_HALO
    return pl.pallas_call(
        _conv_kernel,
        grid=(B, S // CONV_TILE),
        in_specs=[
            pl.BlockSpec((None, CONV_HALO, CONV_CH), lambda b, s: (b, jnp.maximum(s * per - 1, 0), 0)),
            pl.BlockSpec((None, CONV_TILE, CONV_CH), lambda b, s: (b, s, 0)),
            _const_spec((CONV_WIDTH, SUBLANES, CONV_CH)),
            _const_spec((1, CONV_CH)),
            _const_spec((1, CONV_CH)),
            _const_spec((1, CONV_CH)),
        ],
        out_specs=pl.BlockSpec((None, CONV_TILE, CONV_CH), lambda b, s: (b, s, 0)),
        out_shape=jax.ShapeDtypeStruct((B, S, CONV_CH), BF16),
        scratch_shapes=[pltpu.VMEM((SUBLANES, CONV_HALO + CONV_TILE, CONV_CH), F32),
                        pltpu.VMEM((2, CONV_ROWS, CONV_CH), F32)],
        compiler_params=pltpu.CompilerParams(dimension_semantics=("arbitrary",) * 2, vmem_limit_bytes=VMEM_LIMIT),
        name="conformer_conv",
    )(h3, h3, w, cb, lg, lb)


def _post_head(x_ref, tok_ref, qm_ref, km_ref, vm_ref, wo_ref, g2_ref, o_ref, memo_ref, h2_ref):
    for p in range(MEM_WIDTH // LANES):
        cols = slice(p * LANES, (p + 1) * LANES)
        q2 = _stack_pair(qm_ref[:, cols])
        s = _nt_dot(q2, km_ref[:, cols])
        m = jnp.max(s, axis=-1, keepdims=True)
        e = jnp.exp2(s - m)
        l = jnp.sum(e, axis=-1, keepdims=True)
        o2 = jnp.dot(e.astype(BF16), vm_ref[:, cols], preferred_element_type=F32) * (1.0 / l)
        memo_ref[:, cols] = _unstack_pair(o2).astype(BF16)
    y = jnp.dot(tok_ref[...], wo_ref[0:TOK_WIDTH, :], preferred_element_type=F32)
    y = y + jnp.dot(memo_ref[...], wo_ref[TOK_WIDTH:D_MODEL, :], preferred_element_type=F32)
    x1 = x_ref[...] + y
    o_ref[...] = x1
    h2_ref[...] = _rms(x1, g2_ref[...]).astype(BF16)


def _ffn_chunk(h2_ref, wg, wu, wd, o_ref):
    h2 = h2_ref[...]
    g = jnp.dot(h2, wg, preferred_element_type=F32)
    u = jnp.dot(h2, wu, preferred_element_type=F32)
    a = (g * jax.nn.sigmoid(g) * u).astype(BF16)
    o_ref[...] += jnp.dot(a, wd, preferred_element_type=F32)


def _post_a_kernel(x_ref, tok_ref, qm_ref, km_ref, vm_ref, wo_ref, g2_ref, wg_ref, wu_ref, wd_ref,
                   o_ref, memo_ref, h2_ref):
    _post_head(x_ref, tok_ref, qm_ref, km_ref, vm_ref, wo_ref, g2_ref, o_ref, memo_ref, h2_ref)
    for c in range(D_FF // FF_CHUNK):
        cs = slice(c * FF_CHUNK, (c + 1) * FF_CHUNK)
        _ffn_chunk(h2_ref, wg_ref[:, cs], wu_ref[:, cs], wd_ref[cs, :], o_ref)


def _post_specs(T, n_seq, qm_block):
    tiles_per_seq = (T // n_seq) // TOK_TILE
    in_specs = [
        pl.BlockSpec((TOK_TILE, D_MODEL), lambda i: (i, 0)),
        pl.BlockSpec((TOK_TILE, MEM_WIDTH), lambda i: (i, qm_block)),
        pl.BlockSpec((None, MEM_TOKENS, MEM_WIDTH), lambda i: (i // tiles_per_seq, 0, 0)),
        pl.BlockSpec((None, MEM_TOKENS, MEM_WIDTH), lambda i: (i // tiles_per_seq, 0, 0)),
        _const_spec((D_MODEL, D_MODEL)),
        _const_spec((1, D_MODEL)),
        _const_spec((D_MODEL, D_FF)),
        _const_spec((D_MODEL, D_FF)),
        _const_spec((D_FF, D_MODEL)),
    ]
    scratch = [pltpu.VMEM((TOK_TILE, MEM_WIDTH), BF16), pltpu.VMEM((TOK_TILE, D_MODEL), BF16)]
    return tiles_per_seq, in_specs, scratch


def _post_a(x2, tok2, qm2, km, vm, wo, g2, wg, wu, wd, qm_block):
    T = x2.shape[0]
    _, in_specs, scratch = _post_specs(T, km.shape[0], qm_block)
    in_specs.insert(1, pl.BlockSpec((TOK_TILE, TOK_WIDTH), lambda i: (i, 0)))
    return pl.pallas_call(
        _post_a_kernel,
        grid=(T // TOK_TILE,),
        in_specs=in_specs,
        out_specs=pl.BlockSpec((TOK_TILE, D_MODEL), lambda i: (i, 0)),
        out_shape=jax.ShapeDtypeStruct((T, D_MODEL), F32),
        scratch_shapes=scratch,
        compiler_params=pltpu.CompilerParams(dimension_semantics=("arbitrary",), vmem_limit_bytes=VMEM_LIMIT),
        name="post_a",
    )(x2, tok2, qm2, km, vm, wo, g2, wg, wu, wd)


def _row(v):
    return v.reshape(1, -1).astype(F32)


def _tile_heads(g, width, scale=1.0):
    return _row(jnp.tile(g * scale, width // HEAD_DIM))


def kernel(x, mem, norm1_g, mem_norm_g, a_w_in, a_q_g, a_k_g, a_rel_bias, b_w_in, b_b_in, b_conv_w, b_conv_b,
           b_ln_g, b_ln_b, mq_g, mk_g, w_mem_kv, w_out, norm2_g, w_gate, w_up, w_down):
    B, S, D = x.shape
    T = B * S
    depth = norm1_g.shape[0]
    x2 = x.reshape(T, D)
    for i in range(depth):
        j = i // 2
        km, vm = _mem_kv(mem, _row(mem_norm_g[i]), w_mem_kv[i].astype(BF16), _tile_heads(mk_g[i], MEM_WIDTH))
        gm = _tile_heads(mq_g[i], MEM_WIDTH, ATTN_SCALE * LOG2E)
        ff = (w_out[i].astype(BF16), _row(norm2_g[i]), w_gate[i].astype(BF16), w_up[i].astype(BF16),
              w_down[i].astype(BF16))
        if i % 2 == 0:
            z = _in_proj_a(x2, _row(norm1_g[i]), a_w_in[j].astype(BF16),
                           _tile_heads(a_q_g[j], COL_CHUNK, ATTN_SCALE * LOG2E), _tile_heads(a_k_g[j], COL_CHUNK), gm)
            tok = _band_attention(z.reshape(B, S, A_IN), _band_bias(a_rel_bias[j])).reshape(T, TOK_WIDTH)
            x2 = _post_a(x2, tok, z, km, vm, *ff, (3 * TOK_WIDTH) // MEM_WIDTH)
        else:
            h, qm2 = _in_proj_b(x2, _row(norm1_g[i]), b_w_in[j].astype(BF16), _row(b_b_in[j]), gm)
            conv_w = jnp.broadcast_to(b_conv_w[j].astype(F32)[:, None, :], (CONV_WIDTH, SUBLANES, CONV_CH))
            tok = _conformer_conv(h.reshape(B, S, CONV_CH), conv_w, _row(b_conv_b[j]),
                                  _row(b_ln_g[j]), _row(b_ln_b[j])).reshape(T, CONV_CH)
            x2 = _post_a(x2, tok, qm2, km, vm, *ff, 0)
    return x2.reshape(B, S, D)
```

```python
import jax
import jax.numpy as jnp
from jax import lax
from jax.experimental import pallas as pl
from jax.experimental.pallas import tpu as pltpu

D_MODEL = 1024
CHUNK = 64
HEAD_DIM = 64
MEM_TOKENS = 256
MEM_HEADS = 4
MEM_WIDTH = MEM_HEADS * HEAD_DIM
TOK_WIDTH = D_MODEL - MEM_WIDTH
ATT_HEADS = TOK_WIDTH // HEAD_DIM
LEFT_CHUNKS = 8
BAND = (LEFT_CHUNKS + 1) * CHUNK
BAND_PAD = LEFT_CHUNKS * CHUNK
REL_CLIP = 128
CONV_WIDTH = 31
CONV_CH = TOK_WIDTH
A_IN = 3 * TOK_WIDTH + MEM_WIDTH
B_IN = 2 * CONV_CH + MEM_WIDTH
D_FF = 2816
EPS = 1e-6
NEG_INF = -1e30
ATTN_SCALE = HEAD_DIM ** -0.5

LANES = 128
PAIR = 2 * HEAD_DIM
TOK_TILE = 512
ATT_BLK = 4 * CHUNK
ATT_KEYS = ATT_BLK + BAND_PAD
ATT_ROWS = 32
LOG2E = 1.4426950408889634
COL_CHUNK = 256
FF_CHUNK = 256
CONV_HALO = 32
CONV_TILE = 512
CONV_ROWS = 64
SUBLANES = 8
VMEM_LIMIT = 56 * 1024 * 1024

BF16 = jnp.bfloat16
F32 = jnp.float32


def _const_spec(shape):
    return pl.BlockSpec(shape, lambda *_: (0,) * len(shape), pipeline_mode=pl.Buffered(1))


def _layer_spec(shape, layer):
    return pl.BlockSpec((None,) + shape, lambda *_: (layer,) + (0,) * len(shape), pipeline_mode=pl.Buffered(1))


def _rms(x, g):
    return x * lax.rsqrt(jnp.mean(x * x, axis=-1, keepdims=True) + EPS) * g


def _left_mask(shape):
    return lax.broadcasted_iota(jnp.int32, shape, len(shape) - 1) < HEAD_DIM


def _pair_head_norm(z, g):
    left = _left_mask(z.shape)
    sq = z * z
    ss_l = jnp.sum(jnp.where(left, sq, 0.0), axis=-1, keepdims=True)
    ss_r = jnp.sum(jnp.where(left, 0.0, sq), axis=-1, keepdims=True)
    r = jnp.where(left, lax.rsqrt(ss_l * (1.0 / HEAD_DIM) + EPS), lax.rsqrt(ss_r * (1.0 / HEAD_DIM) + EPS))
    return z * r * g


def _head_norm(z, g):
    parts = [_pair_head_norm(z[:, p * LANES:(p + 1) * LANES], g[:, p * LANES:(p + 1) * LANES])
             for p in range(z.shape[1] // LANES)]
    return jnp.concatenate(parts, axis=1) if len(parts) > 1 else parts[0]


def _stack_pair(q):
    left = _left_mask(q.shape)
    zero = jnp.zeros_like(q)
    return jnp.concatenate([jnp.where(left, q, zero), jnp.where(left, zero, q)], axis=0)


def _unstack_pair(o2):
    rows = o2.shape[0] // 2
    left = _left_mask((rows, LANES))
    return jnp.where(left, o2[:rows], o2[rows:])


def _nt_dot(a, b):
    return lax.dot_general(a, b, (((1,), (1,)), ((), ())), preferred_element_type=F32)


def _mem_kv_kernel(mem_ref, g_ref, w_ref, gk_ref, k_ref, v_ref):
    m = _rms(mem_ref[...], g_ref[...]).astype(BF16)
    kv = jnp.dot(m, w_ref[...], preferred_element_type=F32)
    k_ref[...] = _head_norm(kv[:, :MEM_WIDTH], gk_ref[...]).astype(BF16)
    v_ref[...] = kv[:, MEM_WIDTH:].astype(BF16)


def _mem_kv(mem, g, w, gk, layer):
    B = mem.shape[0]
    return pl.pallas_call(
        _mem_kv_kernel,
        grid=(B,),
        in_specs=[
            pl.BlockSpec((None, MEM_TOKENS, D_MODEL), lambda b: (b, 0, 0)),
            _const_spec((1, D_MODEL)),
            _layer_spec((D_MODEL, 2 * MEM_WIDTH), layer),
            _const_spec((1, MEM_WIDTH)),
        ],
        out_specs=[
            pl.BlockSpec((None, MEM_TOKENS, MEM_WIDTH), lambda b: (b, 0, 0)),
            pl.BlockSpec((None, MEM_TOKENS, MEM_WIDTH), lambda b: (b, 0, 0)),
        ],
        out_shape=[jax.ShapeDtypeStruct((B, MEM_TOKENS, MEM_WIDTH), BF16)] * 2,
        compiler_params=pltpu.CompilerParams(dimension_semantics=("arbitrary",)),
        name="mem_kv",
    )(mem, g, w, gk)


def _in_a_kernel(x_ref, g1_ref, w_ref, gq_ref, gk_ref, gm_ref, o_ref):
    h = _rms(x_ref[...], g1_ref[...]).astype(BF16)
    n_tok = TOK_WIDTH // COL_CHUNK
    for j in range(A_IN // COL_CHUNK):
        cols = slice(j * COL_CHUNK, (j + 1) * COL_CHUNK)
        z = jnp.dot(h, w_ref[:, cols], preferred_element_type=F32)
        if j < n_tok:
            z = _head_norm(z, gq_ref[...])
        elif j < 2 * n_tok:
            z = _head_norm(z, gk_ref[...])
        elif j >= 3 * n_tok:
            z = _head_norm(z, gm_ref[...])
        o_ref[:, cols] = z.astype(BF16)


def _in_proj_a(x2, g1, w, gq, gk, gm):
    T = x2.shape[0]
    return pl.pallas_call(
        _in_a_kernel,
        grid=(T // TOK_TILE,),
        in_specs=[
            pl.BlockSpec((TOK_TILE, D_MODEL), lambda i: (i, 0)),
            _const_spec((1, D_MODEL)),
            _const_spec((D_MODEL, A_IN)),
            _const_spec((1, COL_CHUNK)),
            _const_spec((1, COL_CHUNK)),
            _const_spec((1, COL_CHUNK)),
        ],
        out_specs=pl.BlockSpec((TOK_TILE, A_IN), lambda i: (i, 0)),
        out_shape=jax.ShapeDtypeStruct((T, A_IN), BF16),
        compiler_params=pltpu.CompilerParams(dimension_semantics=("arbitrary",), vmem_limit_bytes=VMEM_LIMIT),
        name="in_proj_a",
    )(x2, g1, w, gq, gk, gm)


def _in_b_kernel(x_ref, g1_ref, w_ref, b_ref, gm_ref, h_ref, qm_ref):
    h = _rms(x_ref[...], g1_ref[...]).astype(BF16)
    n_ch = CONV_CH // COL_CHUNK
    for j in range(n_ch):
        ca = slice(j * COL_CHUNK, (j + 1) * COL_CHUNK)
        cg = slice(CONV_CH + j * COL_CHUNK, CONV_CH + (j + 1) * COL_CHUNK)
        a = jnp.dot(h, w_ref[:, ca], preferred_element_type=F32) + b_ref[:, ca]
        gate = jnp.dot(h, w_ref[:, cg], preferred_element_type=F32) + b_ref[:, cg]
        h_ref[:, ca] = a * jax.nn.sigmoid(gate)
    cm = slice(2 * CONV_CH, B_IN)
    zm = jnp.dot(h, w_ref[:, cm], preferred_element_type=F32) + b_ref[:, cm]
    qm_ref[...] = _head_norm(zm, gm_ref[...]).astype(BF16)


def _in_proj_b(x2, g1, w, b, gm):
    T = x2.shape[0]
    return pl.pallas_call(
        _in_b_kernel,
        grid=(T // TOK_TILE,),
        in_specs=[
            pl.BlockSpec((TOK_TILE, D_MODEL), lambda i: (i, 0)),
            _const_spec((1, D_MODEL)),
            _const_spec((D_MODEL, B_IN)),
            _const_spec((1, B_IN)),
            _const_spec((1, MEM_WIDTH)),
        ],
        out_specs=[
            pl.BlockSpec((TOK_TILE, CONV_CH), lambda i: (i, 0)),
            pl.BlockSpec((TOK_TILE, MEM_WIDTH), lambda i: (i, 0)),
        ],
        out_shape=[jax.ShapeDtypeStruct((T, CONV_CH), F32), jax.ShapeDtypeStruct((T, MEM_WIDTH), BF16)],
        compiler_params=pltpu.CompilerParams(dimension_semantics=("arbitrary",), vmem_limit_bytes=VMEM_LIMIT),
        name="in_proj_b",
    )(x2, g1, w, b, gm)


def _attn_kernel(q_ref, k_ref, v_ref, bias_ref, o_ref, s_ref, p_ref, va_ref):
    seq = q_ref.shape[0]
    lane = lax.broadcasted_iota(jnp.int32, (seq, LANES), 1)
    va_ref[:, 0:LANES] = v_ref[...]
    va_ref[:, LANES:2 * LANES] = jnp.where(lane == 0, 1.0, 0.0).astype(BF16)
    for i in range(seq // ATT_BLK):
        q0 = i * ATT_BLK
        k_lo = max(q0 - BAND_PAD, 0)
        k_hi = q0 + ATT_BLK
        n_keys = k_hi - k_lo
        b_lo = ATT_KEYS - n_keys
        q2 = _stack_pair(q_ref[q0:q0 + ATT_BLK, :])
        s_ref[:, 0:n_keys] = _nt_dot(q2, k_ref[k_lo:k_hi, :])
        for r in range(0, 2 * ATT_BLK, ATT_ROWS):
            s = s_ref[r:r + ATT_ROWS, 0:n_keys] + bias_ref[r:r + ATT_ROWS, b_lo:ATT_KEYS]
            m = jnp.max(s, axis=-1, keepdims=True)
            p_ref[r:r + ATT_ROWS, 0:n_keys] = jnp.exp2(s - m).astype(BF16)
        o2 = jnp.dot(p_ref[:, 0:n_keys], va_ref[k_lo:k_hi, :], preferred_element_type=F32)
        o2 = o2[:, 0:LANES] * (1.0 / o2[:, LANES:LANES + 1])
        o_ref[q0:q0 + ATT_BLK, :] = _unstack_pair(o2).astype(BF16)


def _band_attention(z3, bias):
    B, S, _ = z3.shape
    n_pairs = TOK_WIDTH // LANES
    blk = (None, S, LANES)
    return pl.pallas_call(
        _attn_kernel,
        grid=(n_pairs, B),
        in_specs=[
            pl.BlockSpec(blk, lambda p, b: (b, 0, p)),
            pl.BlockSpec(blk, lambda p, b: (b, 0, n_pairs + p)),
            pl.BlockSpec(blk, lambda p, b: (b, 0, 2 * n_pairs + p)),
            pl.BlockSpec((None, 2 * ATT_BLK, ATT_KEYS), lambda p, b: (p, 0, 0)),
        ],
        out_specs=pl.BlockSpec(blk, lambda p, b: (b, 0, p)),
        out_shape=jax.ShapeDtypeStruct((B, S, TOK_WIDTH), BF16),
        scratch_shapes=[
            pltpu.VMEM((2 * ATT_BLK, ATT_KEYS), F32),
            pltpu.VMEM((2 * ATT_BLK, ATT_KEYS), BF16),
            pltpu.VMEM((S, 2 * LANES), BF16),
        ],
        compiler_params=pltpu.CompilerParams(dimension_semantics=("arbitrary",) * 2),
        name="band_attn",
    )(z3, z3, z3, bias)


def _band_bias(rel_bias):
    n_rel = rel_bias.shape[1]
    ext = jnp.concatenate(
        [jnp.broadcast_to(rel_bias[:, n_rel - 1:], (ATT_HEADS, BAND - REL_CLIP)), rel_bias[:, n_rel - 2::-1]], axis=1)
    chunk = jnp.stack([ext[:, CHUNK - 1 - qi:CHUNK - 1 - qi + BAND] for qi in range(CHUNK)], axis=1)
    n_cq = ATT_BLK // CHUNK
    blk = jnp.stack(
        [jnp.pad(chunk, ((0, 0), (0, 0), (cq * CHUNK, ATT_KEYS - BAND - cq * CHUNK)), constant_values=NEG_INF)
         for cq in range(n_cq)], axis=1)
    return (blk * LOG2E).reshape(ATT_HEADS // 2, 2 * ATT_BLK, ATT_KEYS).astype(F32)


def _conv_kernel(halo_ref, h_ref, w_ref, cb_ref, lg_ref, lb_ref, o_ref, hs_ref, y_ref):
    first = pl.program_id(1) == 0
    hs_ref[0, 0:CONV_HALO, :] = jnp.where(first, 0.0, halo_ref[...])
    hs_ref[0, CONV_HALO:CONV_HALO + CONV_TILE, :] = h_ref[...]
    n_shift = hs_ref.shape[1] - SUBLANES
    for b in range(1, SUBLANES):
        hs_ref[b, 0:n_shift, :] = hs_ref[0, b:b + n_shift, :]
    base = CONV_HALO - (CONV_WIDTH - 1)

    n_blocks = CONV_TILE // CONV_ROWS

    def conv_rows(rb):
        r = rb * CONV_ROWS if isinstance(rb, int) else pl.multiple_of(rb * CONV_ROWS, CONV_ROWS)
        for cb in range(CONV_CH // LANES):
            cols = slice(cb * LANES, (cb + 1) * LANES)
            n_acc = CONV_ROWS // SUBLANES
            acc = [jnp.zeros((SUBLANES, LANES), F32)] * n_acc
            for b in range(SUBLANES):
                taps = [(j, (base + j) // SUBLANES) for j in range(CONV_WIDTH) if (base + j) % SUBLANES == b]
                w_of = {a: w_ref[j, :, cols] for j, a in taps}
                for g in range(min(w_of), n_acc + max(w_of)):
                    xg = hs_ref[b, pl.ds(r + g * SUBLANES, SUBLANES), cols]
                    for a, wj in w_of.items():
                        if 0 <= g - a < n_acc:
                            acc[g - a] = acc[g - a] + xg * wj
            for k in range(CONV_ROWS // SUBLANES):
                y_ref[rb % 2, k * SUBLANES:(k + 1) * SUBLANES, cols] = acc[k] + cb_ref[:, cols]

    def norm_rows(rb):
        r = rb * CONV_ROWS if isinstance(rb, int) else pl.multiple_of(rb * CONV_ROWS, CONV_ROWS)
        y = y_ref[rb % 2]
        mu = jnp.mean(y, axis=-1, keepdims=True)
        yc = y - mu
        yn = yc * lax.rsqrt(jnp.mean(yc * yc, axis=-1, keepdims=True) + EPS) * lg_ref[...] + lb_ref[...]
        o_ref[pl.ds(r, CONV_ROWS), :] = (yn * jax.nn.sigmoid(yn)).astype(BF16)

    def step(rb, carry):
        norm_rows(rb - 1)
        conv_rows(rb)
        return carry

    conv_rows(0)
    lax.fori_loop(1, n_blocks, step, 0)
    norm_rows(n_blocks - 1)


def _conformer_conv(h3, w, cb, lg, lb):
    B, S, _ = h3.shape
    per = CONV_TILE // CONV_HALO
    return pl.pallas_call(
        _conv_kernel,
        grid=(B, S // CONV_TILE),
        in_specs=[
            pl.BlockSpec((None, CONV_HALO, CONV_CH), lambda b, s: (b, jnp.maximum(s * per - 1, 0), 0)),
            pl.BlockSpec((None, CONV_TILE, CONV_CH), lambda b, s: (b, s, 0)),
            _const_spec((CONV_WIDTH, SUBLANES, CONV_CH)),
            _const_spec((1, CONV_CH)),
            _const_spec((1, CONV_CH)),
            _const_spec((1, CONV_CH)),
        ],
        out_specs=pl.BlockSpec((None, CONV_TILE, CONV_CH), lambda b, s: (b, s, 0)),
        out_shape=jax.ShapeDtypeStruct((B, S, CONV_CH), BF16),
        scratch_shapes=[pltpu.VMEM((SUBLANES, CONV_HALO + CONV_TILE, CONV_CH), F32),
                        pltpu.VMEM((2, CONV_ROWS, CONV_CH), F32)],
        compiler_params=pltpu.CompilerParams(dimension_semantics=("arbitrary",) * 2, vmem_limit_bytes=VMEM_LIMIT),
        name="conformer_conv",
    )(h3, h3, w, cb, lg, lb)


def _post_head(x_ref, tok_ref, qm_ref, km_ref, vm_ref, wo_ref, g2_ref, o_ref, memo_ref, h2_ref):
    for p in range(MEM_WIDTH // LANES):
        cols = slice(p * LANES, (p + 1) * LANES)
        q2 = _stack_pair(qm_ref[:, cols])
        s = _nt_dot(q2, km_ref[:, cols])
        m = jnp.max(s, axis=-1, keepdims=True)
        e = jnp.exp2(s - m)
        l = jnp.sum(e, axis=-1, keepdims=True)
        o2 = jnp.dot(e.astype(BF16), vm_ref[:, cols], preferred_element_type=F32) * (1.0 / l)
        memo_ref[:, cols] = _unstack_pair(o2).astype(BF16)
    y = jnp.dot(tok_ref[...], wo_ref[0:TOK_WIDTH, :], preferred_element_type=F32)
    y = y + jnp.dot(memo_ref[...], wo_ref[TOK_WIDTH:D_MODEL, :], preferred_element_type=F32)
    x1 = x_ref[...] + y
    o_ref[...] = x1
    h2_ref[...] = _rms(x1, g2_ref[...]).astype(BF16)


def _ffn_chunk(h2_ref, wg, wu, wd, o_ref):
    h2 = h2_ref[...]
    g = jnp.dot(h2, wg, preferred_element_type=F32)
    u = jnp.dot(h2, wu, preferred_element_type=F32)
    a = (g * jax.nn.sigmoid(g) * u).astype(BF16)
    o_ref[...] += jnp.dot(a, wd, preferred_element_type=F32)


def _post_a_kernel(x_ref, tok_ref, qm_ref, km_ref, vm_ref, wo_ref, g2_ref, wg_ref, wu_ref, wd_ref,
                   o_ref, memo_ref, h2_ref):
    _post_head(x_ref, tok_ref, qm_ref, km_ref, vm_ref, wo_ref, g2_ref, o_ref, memo_ref, h2_ref)
    for c in range(D_FF // FF_CHUNK):
        cs = slice(c * FF_CHUNK, (c + 1) * FF_CHUNK)
        _ffn_chunk(h2_ref, wg_ref[:, cs], wu_ref[:, cs], wd_ref[cs, :], o_ref)


def _post_specs(T, n_seq, qm_block, layer):
    tiles_per_seq = (T // n_seq) // TOK_TILE
    in_specs = [
        pl.BlockSpec((TOK_TILE, D_MODEL), lambda i: (i, 0)),
        pl.BlockSpec((TOK_TILE, MEM_WIDTH), lambda i: (i, qm_block)),
        pl.BlockSpec((None, MEM_TOKENS, MEM_WIDTH), lambda i: (i // tiles_per_seq, 0, 0)),
        pl.BlockSpec((None, MEM_TOKENS, MEM_WIDTH), lambda i: (i // tiles_per_seq, 0, 0)),
        _layer_spec((D_MODEL, D_MODEL), layer),
        _const_spec((1, D_MODEL)),
        _layer_spec((D_MODEL, D_FF), layer),
        _layer_spec((D_MODEL, D_FF), layer),
        _layer_spec((D_FF, D_MODEL), layer),
    ]
    scratch = [pltpu.VMEM((TOK_TILE, MEM_WIDTH), BF16), pltpu.VMEM((TOK_TILE, D_MODEL), BF16)]
    return tiles_per_seq, in_specs, scratch


def _post_a(x2, tok2, qm2, km, vm, wo, g2, wg, wu, wd, qm_block, layer):
    T = x2.shape[0]
    _, in_specs, scratch = _post_specs(T, km.shape[0], qm_block, layer)
    in_specs.insert(1, pl.BlockSpec((TOK_TILE, TOK_WIDTH), lambda i: (i, 0)))
    return pl.pallas_call(
        _post_a_kernel,
        grid=(T // TOK_TILE,),
        in_specs=in_specs,
        out_specs=pl.BlockSpec((TOK_TILE, D_MODEL), lambda i: (i, 0)),
        out_shape=jax.ShapeDtypeStruct((T, D_MODEL), F32),
        scratch_shapes=scratch,
        compiler_params=pltpu.CompilerParams(dimension_semantics=("arbitrary",), vmem_limit_bytes=VMEM_LIMIT),
        name="post_a",
    )(x2, tok2, qm2, km, vm, wo, g2, wg, wu, wd)


def _row(v):
    return v.reshape(1, -1).astype(F32)


def _tile_heads(g, width, scale=1.0):
    return _row(jnp.tile(g * scale, width // HEAD_DIM))


def kernel(x, mem, norm1_g, mem_norm_g, a_w_in, a_q_g, a_k_g, a_rel_bias, b_w_in, b_b_in, b_conv_w, b_conv_b,
           b_ln_g, b_ln_b, mq_g, mk_g, w_mem_kv, w_out, norm2_g, w_gate, w_up, w_down):
    B, S, D = x.shape
    T = B * S
    depth = norm1_g.shape[0]
    x2 = x.reshape(T, D)
    w_mem_kv, w_out, w_gate, w_up, w_down = (w.astype(BF16) for w in (w_mem_kv, w_out, w_gate, w_up, w_down))
    for i in range(depth):
        j = i // 2
        km, vm = _mem_kv(mem, _row(mem_norm_g[i]), w_mem_kv, _tile_heads(mk_g[i], MEM_WIDTH), i)
        gm = _tile_heads(mq_g[i], MEM_WIDTH, ATTN_SCALE * LOG2E)
        ff = (w_out, _row(norm2_g[i]), w_gate, w_up, w_down)
        if i % 2 == 0:
            z = _in_proj_a(x2, _row(norm1_g[i]), a_w_in[j].astype(BF16),
                           _tile_heads(a_q_g[j], COL_CHUNK, ATTN_SCALE * LOG2E), _tile_heads(a_k_g[j], COL_CHUNK), gm)
            tok = _band_attention(z.reshape(B, S, A_IN), _band_bias(a_rel_bias[j])).reshape(T, TOK_WIDTH)
            x2 = _post_a(x2, tok, z, km, vm, *ff, (3 * TOK_WIDTH) // MEM_WIDTH, i)
        else:
            h, qm2 = _in_proj_b(x2, _row(norm1_g[i]), b_w_in[j].astype(BF16), _row(b_b_in[j]), gm)
            conv_w = jnp.broadcast_to(b_conv_w[j].astype(F32)[:, None, :], (CONV_WIDTH, SUBLANES, CONV_CH))
            tok = _conformer_conv(h.reshape(B, S, CONV_CH), conv_w, _row(b_conv_b[j]),
                                  _row(b_ln_g[j]), _row(b_ln_b[j])).reshape(T, CONV_CH)
            x2 = _post_a(x2, tok, qm2, km, vm, *ff, 0, i)
    return x2.reshape(B, S, D)
```

```python
import jax
import jax.numpy as jnp
from jax import lax
from jax.experimental import pallas as pl
from jax.experimental.pallas import tpu as pltpu

D_MODEL = 1024
CHUNK = 64
HEAD_DIM = 64
MEM_TOKENS = 256
MEM_HEADS = 4
MEM_WIDTH = MEM_HEADS * HEAD_DIM
TOK_WIDTH = D_MODEL - MEM_WIDTH
ATT_HEADS = TOK_WIDTH // HEAD_DIM
LEFT_CHUNKS = 8
BAND = (LEFT_CHUNKS + 1) * CHUNK
BAND_PAD = LEFT_CHUNKS * CHUNK
REL_CLIP = 128
CONV_WIDTH = 31
CONV_CH = TOK_WIDTH
A_IN = 3 * TOK_WIDTH + MEM_WIDTH
B_IN = 2 * CONV_CH + MEM_WIDTH
D_FF = 2816
EPS = 1e-6
NEG_INF = -1e30
ATTN_SCALE = HEAD_DIM ** -0.5

LANES = 128
PAIR = 2 * HEAD_DIM
TOK_TILE = 512
ATT_BLK = 4 * CHUNK
ATT_KEYS = ATT_BLK + BAND_PAD
ATT_ROWS = 32
LOG2E = 1.4426950408889634
COL_CHUNK = 256
FF_CHUNK = 256
CONV_HALO = 32
CONV_TILE = 512
CONV_ROWS = 64
SUBLANES = 8
VMEM_LIMIT = 56 * 1024 * 1024

BF16 = jnp.bfloat16
F32 = jnp.float32


def _const_spec(shape):
    return pl.BlockSpec(shape, lambda *_: (0,) * len(shape), pipeline_mode=pl.Buffered(1))


def _layer_spec(shape, layer):
    return pl.BlockSpec((None,) + shape, lambda *_: (layer,) + (0,) * len(shape), pipeline_mode=pl.Buffered(1))


def _rms(x, g):
    return x * lax.rsqrt(jnp.mean(x * x, axis=-1, keepdims=True) + EPS) * g


def _left_mask(shape):
    return lax.broadcasted_iota(jnp.int32, shape, len(shape) - 1) < HEAD_DIM


def _pair_head_norm(z, g):
    left = _left_mask(z.shape)
    sq = z * z
    ss_l = jnp.sum(jnp.where(left, sq, 0.0), axis=-1, keepdims=True)
    ss_r = jnp.sum(jnp.where(left, 0.0, sq), axis=-1, keepdims=True)
    r = jnp.where(left, lax.rsqrt(ss_l * (1.0 / HEAD_DIM) + EPS), lax.rsqrt(ss_r * (1.0 / HEAD_DIM) + EPS))
    return z * r * g


def _head_norm(z, g):
    parts = [_pair_head_norm(z[:, p * LANES:(p + 1) * LANES], g[:, p * LANES:(p + 1) * LANES])
             for p in range(z.shape[1] // LANES)]
    return jnp.concatenate(parts, axis=1) if len(parts) > 1 else parts[0]


def _stack_pair(q):
    left = _left_mask(q.shape)
    zero = jnp.zeros_like(q)
    return jnp.concatenate([jnp.where(left, q, zero), jnp.where(left, zero, q)], axis=0)


def _unstack_pair(o2):
    rows = o2.shape[0] // 2
    left = _left_mask((rows, LANES))
    return jnp.where(left, o2[:rows], o2[rows:])


def _nt_dot(a, b):
    return lax.dot_general(a, b, (((1,), (1,)), ((), ())), preferred_element_type=F32)


def _mem_kv_kernel(mem_ref, g_ref, w_ref, gk_ref, k_ref, v_ref):
    m = _rms(mem_ref[...], g_ref[...]).astype(BF16)
    kv = jnp.dot(m, w_ref[...].astype(BF16), preferred_element_type=F32)
    k_ref[...] = _head_norm(kv[:, :MEM_WIDTH], gk_ref[...]).astype(BF16)
    v_ref[...] = kv[:, MEM_WIDTH:].astype(BF16)


def _mem_kv(mem, g, w, gk, layer):
    B = mem.shape[0]
    return pl.pallas_call(
        _mem_kv_kernel,
        grid=(B,),
        in_specs=[
            pl.BlockSpec((None, MEM_TOKENS, D_MODEL), lambda b: (b, 0, 0)),
            _const_spec((1, D_MODEL)),
            _layer_spec((D_MODEL, 2 * MEM_WIDTH), layer),
            _const_spec((1, MEM_WIDTH)),
        ],
        out_specs=[
            pl.BlockSpec((None, MEM_TOKENS, MEM_WIDTH), lambda b: (b, 0, 0)),
            pl.BlockSpec((None, MEM_TOKENS, MEM_WIDTH), lambda b: (b, 0, 0)),
        ],
        out_shape=[jax.ShapeDtypeStruct((B, MEM_TOKENS, MEM_WIDTH), BF16)] * 2,
        compiler_params=pltpu.CompilerParams(dimension_semantics=("arbitrary",)),
        name="mem_kv",
    )(mem, g, w, gk)


def _in_a_kernel(x_ref, g1_ref, w_ref, gq_ref, gk_ref, gm_ref, o_ref):
    h = _rms(x_ref[...], g1_ref[...]).astype(BF16)
    n_tok = TOK_WIDTH // COL_CHUNK
    for j in range(A_IN // COL_CHUNK):
        cols = slice(j * COL_CHUNK, (j + 1) * COL_CHUNK)
        z = jnp.dot(h, w_ref[:, cols].astype(BF16), preferred_element_type=F32)
        if j < n_tok:
            z = _head_norm(z, gq_ref[...])
        elif j < 2 * n_tok:
            z = _head_norm(z, gk_ref[...])
        elif j >= 3 * n_tok:
            z = _head_norm(z, gm_ref[...])
        o_ref[:, cols] = z.astype(BF16)


def _in_proj_a(x2, g1, w, gq, gk, gm):
    T = x2.shape[0]
    return pl.pallas_call(
        _in_a_kernel,
        grid=(T // TOK_TILE,),
        in_specs=[
            pl.BlockSpec((TOK_TILE, D_MODEL), lambda i: (i, 0)),
            _const_spec((1, D_MODEL)),
            _const_spec((D_MODEL, A_IN)),
            _const_spec((1, COL_CHUNK)),
            _const_spec((1, COL_CHUNK)),
            _const_spec((1, COL_CHUNK)),
        ],
        out_specs=pl.BlockSpec((TOK_TILE, A_IN), lambda i: (i, 0)),
        out_shape=jax.ShapeDtypeStruct((T, A_IN), BF16),
        compiler_params=pltpu.CompilerParams(dimension_semantics=("arbitrary",), vmem_limit_bytes=VMEM_LIMIT),
        name="in_proj_a",
    )(x2, g1, w, gq, gk, gm)


def _in_b_kernel(x_ref, g1_ref, w_ref, b_ref, gm_ref, h_ref, qm_ref):
    h = _rms(x_ref[...], g1_ref[...]).astype(BF16)
    n_ch = CONV_CH // COL_CHUNK
    for j in range(n_ch):
        ca = slice(j * COL_CHUNK, (j + 1) * COL_CHUNK)
        cg = slice(CONV_CH + j * COL_CHUNK, CONV_CH + (j + 1) * COL_CHUNK)
        a = jnp.dot(h, w_ref[:, ca].astype(BF16), preferred_element_type=F32) + b_ref[:, ca]
        gate = jnp.dot(h, w_ref[:, cg].astype(BF16), preferred_element_type=F32) + b_ref[:, cg]
        h_ref[:, ca] = a * jax.nn.sigmoid(gate)
    cm = slice(2 * CONV_CH, B_IN)
    zm = jnp.dot(h, w_ref[:, cm].astype(BF16), preferred_element_type=F32) + b_ref[:, cm]
    qm_ref[...] = _head_norm(zm, gm_ref[...]).astype(BF16)


def _in_proj_b(x2, g1, w, b, gm):
    T = x2.shape[0]
    return pl.pallas_call(
        _in_b_kernel,
        grid=(T // TOK_TILE,),
        in_specs=[
            pl.BlockSpec((TOK_TILE, D_MODEL), lambda i: (i, 0)),
            _const_spec((1, D_MODEL)),
            _const_spec((D_MODEL, B_IN)),
            _const_spec((1, B_IN)),
            _const_spec((1, MEM_WIDTH)),
        ],
        out_specs=[
            pl.BlockSpec((TOK_TILE, CONV_CH), lambda i: (i, 0)),
            pl.BlockSpec((TOK_TILE, MEM_WIDTH), lambda i: (i, 0)),
        ],
        out_shape=[jax.ShapeDtypeStruct((T, CONV_CH), F32), jax.ShapeDtypeStruct((T, MEM_WIDTH), BF16)],
        compiler_params=pltpu.CompilerParams(dimension_semantics=("arbitrary",), vmem_limit_bytes=VMEM_LIMIT),
        name="in_proj_b",
    )(x2, g1, w, b, gm)


def _attn_kernel(q_ref, k_ref, v_ref, bias_ref, o_ref, s_ref, p_ref, va_ref):
    seq = q_ref.shape[0]
    lane = lax.broadcasted_iota(jnp.int32, (seq, LANES), 1)
    va_ref[:, 0:LANES] = v_ref[...]
    va_ref[:, LANES:2 * LANES] = jnp.where(lane == 0, 1.0, 0.0).astype(BF16)
    for i in range(seq // ATT_BLK):
        q0 = i * ATT_BLK
        k_lo = max(q0 - BAND_PAD, 0)
        k_hi = q0 + ATT_BLK
        n_keys = k_hi - k_lo
        b_lo = ATT_KEYS - n_keys
        q2 = _stack_pair(q_ref[q0:q0 + ATT_BLK, :])
        s_ref[:, 0:n_keys] = _nt_dot(q2, k_ref[k_lo:k_hi, :])
        for r in range(0, 2 * ATT_BLK, ATT_ROWS):
            s = s_ref[r:r + ATT_ROWS, 0:n_keys] + bias_ref[r:r + ATT_ROWS, b_lo:ATT_KEYS]
            m = jnp.max(s, axis=-1, keepdims=True)
            p_ref[r:r + ATT_ROWS, 0:n_keys] = jnp.exp2(s - m).astype(BF16)
        o2 = jnp.dot(p_ref[:, 0:n_keys], va_ref[k_lo:k_hi, :], preferred_element_type=F32)
        o2 = o2[:, 0:LANES] * (1.0 / o2[:, LANES:LANES + 1])
        o_ref[q0:q0 + ATT_BLK, :] = _unstack_pair(o2).astype(BF16)


def _band_attention(z3, bias):
    B, S, _ = z3.shape
    n_pairs = TOK_WIDTH // LANES
    blk = (None, S, LANES)
    return pl.pallas_call(
        _attn_kernel,
        grid=(n_pairs, B),
        in_specs=[
            pl.BlockSpec(blk, lambda p, b: (b, 0, p)),
            pl.BlockSpec(blk, lambda p, b: (b, 0, n_pairs + p)),
            pl.BlockSpec(blk, lambda p, b: (b, 0, 2 * n_pairs + p)),
            pl.BlockSpec((None, 2 * ATT_BLK, ATT_KEYS), lambda p, b: (p, 0, 0)),
        ],
        out_specs=pl.BlockSpec(blk, lambda p, b: (b, 0, p)),
        out_shape=jax.ShapeDtypeStruct((B, S, TOK_WIDTH), BF16),
        scratch_shapes=[
            pltpu.VMEM((2 * ATT_BLK, ATT_KEYS), F32),
            pltpu.VMEM((2 * ATT_BLK, ATT_KEYS), BF16),
            pltpu.VMEM((S, 2 * LANES), BF16),
        ],
        compiler_params=pltpu.CompilerParams(dimension_semantics=("arbitrary",) * 2),
        name="band_attn",
    )(z3, z3, z3, bias)


def _band_bias(rel_bias):
    n_rel = rel_bias.shape[1]
    ext = jnp.concatenate(
        [jnp.broadcast_to(rel_bias[:, n_rel - 1:], (ATT_HEADS, BAND - REL_CLIP)), rel_bias[:, n_rel - 2::-1]], axis=1)
    chunk = jnp.stack([ext[:, CHUNK - 1 - qi:CHUNK - 1 - qi + BAND] for qi in range(CHUNK)], axis=1)
    n_cq = ATT_BLK // CHUNK
    blk = jnp.stack(
        [jnp.pad(chunk, ((0, 0), (0, 0), (cq * CHUNK, ATT_KEYS - BAND - cq * CHUNK)), constant_values=NEG_INF)
         for cq in range(n_cq)], axis=1)
    return (blk * LOG2E).reshape(ATT_HEADS // 2, 2 * ATT_BLK, ATT_KEYS).astype(F32)


def _conv_kernel(halo_ref, h_ref, w_ref, cb_ref, lg_ref, lb_ref, o_ref, hs_ref, y_ref):
    first = pl.program_id(1) == 0
    hs_ref[0, 0:CONV_HALO, :] = jnp.where(first, 0.0, halo_ref[...])
    hs_ref[0, CONV_HALO:CONV_HALO + CONV_TILE, :] = h_ref[...]
    n_shift = hs_ref.shape[1] - SUBLANES
    for b in range(1, SUBLANES):
        hs_ref[b, 0:n_shift, :] = hs_ref[0, b:b + n_shift, :]
    base = CONV_HALO - (CONV_WIDTH - 1)

    n_blocks = CONV_TILE // CONV_ROWS

    def conv_rows(rb):
        r = rb * CONV_ROWS if isinstance(rb, int) else pl.multiple_of(rb * CONV_ROWS, CONV_ROWS)
        for cb in range(CONV_CH // LANES):
            cols = slice(cb * LANES, (cb + 1) * LANES)
            n_acc = CONV_ROWS // SUBLANES
            acc = [jnp.zeros((SUBLANES, LANES), F32)] * n_acc
            for b in range(SUBLANES):
                taps = [(j, (base + j) // SUBLANES) for j in range(CONV_WIDTH) if (base + j) % SUBLANES == b]
                w_of = {a: w_ref[j, :, cols] for j, a in taps}
                for g in range(min(w_of), n_acc + max(w_of)):
                    xg = hs_ref[b, pl.ds(r + g * SUBLANES, SUBLANES), cols]
                    for a, wj in w_of.items():
                        if 0 <= g - a < n_acc:
                            acc[g - a] = acc[g - a] + xg * wj
            for k in range(CONV_ROWS // SUBLANES):
                y_ref[rb % 2, k * SUBLANES:(k + 1) * SUBLANES, cols] = acc[k] + cb_ref[:, cols]

    def norm_rows(rb):
        r = rb * CONV_ROWS if isinstance(rb, int) else pl.multiple_of(rb * CONV_ROWS, CONV_ROWS)
        y = y_ref[rb % 2]
        mu = jnp.mean(y, axis=-1, keepdims=True)
        yc = y - mu
        yn = yc * lax.rsqrt(jnp.mean(yc * yc, axis=-1, keepdims=True) + EPS) * lg_ref[...] + lb_ref[...]
        o_ref[pl.ds(r, CONV_ROWS), :] = (yn * jax.nn.sigmoid(yn)).astype(BF16)

    def step(rb, carry):
        norm_rows(rb - 1)
        conv_rows(rb)
        return carry

    conv_rows(0)
    lax.fori_loop(1, n_blocks, step, 0)
    norm_rows(n_blocks - 1)


def _conformer_conv(h3, w, cb, lg, lb):
    B, S, _ = h3.shape
    per = CONV_TILE // CONV_HALO
    return pl.pallas_call(
        _conv_kernel,
        grid=(B, S // CONV_TILE),
        in_specs=[
            pl.BlockSpec((None, CONV_HALO, CONV_CH), lambda b, s: (b, jnp.maximum(s * per - 1, 0), 0)),
            pl.BlockSpec((None, CONV_TILE, CONV_CH), lambda b, s: (b, s, 0)),
            _const_spec((CONV_WIDTH, SUBLANES, CONV_CH)),
            _const_spec((1, CONV_CH)),
            _const_spec((1, CONV_CH)),
            _const_spec((1, CONV_CH)),
        ],
        out_specs=pl.BlockSpec((None, CONV_TILE, CONV_CH), lambda b, s: (b, s, 0)),
        out_shape=jax.ShapeDtypeStruct((B, S, CONV_CH), BF16),
        scratch_shapes=[pltpu.VMEM((SUBLANES, CONV_HALO + CONV_TILE, CONV_CH), F32),
                        pltpu.VMEM((2, CONV_ROWS, CONV_CH), F32)],
        compiler_params=pltpu.CompilerParams(dimension_semantics=("arbitrary",) * 2, vmem_limit_bytes=VMEM_LIMIT),
        name="conformer_conv",
    )(h3, h3, w, cb, lg, lb)


def _post_head(x_ref, tok_ref, qm_ref, km_ref, vm_ref, wo_ref, g2_ref, o_ref, memo_ref, h2_ref):
    for p in range(MEM_WIDTH // LANES):
        cols = slice(p * LANES, (p + 1) * LANES)
        q2 = _stack_pair(qm_ref[:, cols])
        s = _nt_dot(q2, km_ref[:, cols])
        m = jnp.max(s, axis=-1, keepdims=True)
        e = jnp.exp2(s - m)
        l = jnp.sum(e, axis=-1, keepdims=True)
        o2 = jnp.dot(e.astype(BF16), vm_ref[:, cols], preferred_element_type=F32) * (1.0 / l)
        memo_ref[:, cols] = _unstack_pair(o2).astype(BF16)
    y = jnp.dot(tok_ref[...], wo_ref[0:TOK_WIDTH, :].astype(BF16), preferred_element_type=F32)
    y = y + jnp.dot(memo_ref[...], wo_ref[TOK_WIDTH:D_MODEL, :].astype(BF16), preferred_element_type=F32)
    x1 = x_ref[...] + y
    o_ref[...] = x1
    h2_ref[...] = _rms(x1, g2_ref[...]).astype(BF16)


def _ffn_chunk(h2_ref, wg, wu, wd, o_ref):
    h2 = h2_ref[...]
    g = jnp.dot(h2, wg, preferred_element_type=F32)
    u = jnp.dot(h2, wu, preferred_element_type=F32)
    a = (g * jax.nn.sigmoid(g) * u).astype(BF16)
    o_ref[...] += jnp.dot(a, wd, preferred_element_type=F32)


def _post_a_kernel(x_ref, tok_ref, qm_ref, km_ref, vm_ref, wo_ref, g2_ref, wg_ref, wu_ref, wd_ref,
                   o_ref, memo_ref, h2_ref):
    _post_head(x_ref, tok_ref, qm_ref, km_ref, vm_ref, wo_ref, g2_ref, o_ref, memo_ref, h2_ref)
    for c in range(D_FF // FF_CHUNK):
        cs = slice(c * FF_CHUNK, (c + 1) * FF_CHUNK)
        _ffn_chunk(h2_ref, wg_ref[:, cs].astype(BF16), wu_ref[:, cs].astype(BF16), wd_ref[cs, :].astype(BF16), o_ref)


def _post_specs(T, n_seq, qm_block, layer):
    tiles_per_seq = (T // n_seq) // TOK_TILE
    in_specs = [
        pl.BlockSpec((TOK_TILE, D_MODEL), lambda i: (i, 0)),
        pl.BlockSpec((TOK_TILE, MEM_WIDTH), lambda i: (i, qm_block)),
        pl.BlockSpec((None, MEM_TOKENS, MEM_WIDTH), lambda i: (i // tiles_per_seq, 0, 0)),
        pl.BlockSpec((None, MEM_TOKENS, MEM_WIDTH), lambda i: (i // tiles_per_seq, 0, 0)),
        _layer_spec((D_MODEL, D_MODEL), layer),
        _const_spec((1, D_MODEL)),
        _layer_spec((D_MODEL, D_FF), layer),
        _layer_spec((D_MODEL, D_FF), layer),
        _layer_spec((D_FF, D_MODEL), layer),
    ]
    scratch = [pltpu.VMEM((TOK_TILE, MEM_WIDTH), BF16), pltpu.VMEM((TOK_TILE, D_MODEL), BF16)]
    return tiles_per_seq, in_specs, scratch


def _post_a(x2, tok2, qm2, km, vm, wo, g2, wg, wu, wd, qm_block, layer):
    T = x2.shape[0]
    _, in_specs, scratch = _post_specs(T, km.shape[0], qm_block, layer)
    in_specs.insert(1, pl.BlockSpec((TOK_TILE, TOK_WIDTH), lambda i: (i, 0)))
    return pl.pallas_call(
        _post_a_kernel,
        grid=(T // TOK_TILE,),
        in_specs=in_specs,
        out_specs=pl.BlockSpec((TOK_TILE, D_MODEL), lambda i: (i, 0)),
        out_shape=jax.ShapeDtypeStruct((T, D_MODEL), F32),
        scratch_shapes=scratch,
        compiler_params=pltpu.CompilerParams(dimension_semantics=("arbitrary",), vmem_limit_bytes=VMEM_LIMIT),
        name="post_a",
    )(x2, tok2, qm2, km, vm, wo, g2, wg, wu, wd)


def _row(v):
    return v.reshape(1, -1).astype(F32)


def _tile_heads(g, width, scale=1.0):
    return _row(jnp.tile(g * scale, width // HEAD_DIM))


def kernel(x, mem, norm1_g, mem_norm_g, a_w_in, a_q_g, a_k_g, a_rel_bias, b_w_in, b_b_in, b_conv_w, b_conv_b,
           b_ln_g, b_ln_b, mq_g, mk_g, w_mem_kv, w_out, norm2_g, w_gate, w_up, w_down):
    B, S, D = x.shape
    T = B * S
    depth = norm1_g.shape[0]
    x2 = x.reshape(T, D)
    for i in range(depth):
        j = i // 2
        km, vm = _mem_kv(mem, _row(mem_norm_g[i]), w_mem_kv, _tile_heads(mk_g[i], MEM_WIDTH), i)
        gm = _tile_heads(mq_g[i], MEM_WIDTH, ATTN_SCALE * LOG2E)
        ff = (w_out, _row(norm2_g[i]), w_gate, w_up, w_down)
        if i % 2 == 0:
            z = _in_proj_a(x2, _row(norm1_g[i]), a_w_in[j],
                           _tile_heads(a_q_g[j], COL_CHUNK, ATTN_SCALE * LOG2E), _tile_heads(a_k_g[j], COL_CHUNK), gm)
            tok = _band_attention(z.reshape(B, S, A_IN), _band_bias(a_rel_bias[j])).reshape(T, TOK_WIDTH)
            x2 = _post_a(x2, tok, z, km, vm, *ff, (3 * TOK_WIDTH) // MEM_WIDTH, i)
        else:
            h, qm2 = _in_proj_b(x2, _row(norm1_g[i]), b_w_in[j], _row(b_b_in[j]), gm)
            conv_w = jnp.broadcast_to(b_conv_w[j].astype(F32)[:, None, :], (CONV_WIDTH, SUBLANES, CONV_CH))
            tok = _conformer_conv(h.reshape(B, S, CONV_CH), conv_w, _row(b_conv_b[j]),
                                  _row(b_ln_g[j]), _row(b_ln_b[j])).reshape(T, CONV_CH)
            x2 = _post_a(x2, tok, qm2, km, vm, *ff, 0, i)
    return x2.reshape(B, S, D)
```

```python
import jax
import jax.numpy as jnp
from jax import lax
from jax.experimental import pallas as pl
from jax.experimental.pallas import tpu as pltpu

D_MODEL = 1024
CHUNK = 64
HEAD_DIM = 64
MEM_TOKENS = 256
MEM_HEADS = 4
MEM_WIDTH = MEM_HEADS * HEAD_DIM
TOK_WIDTH = D_MODEL - MEM_WIDTH
ATT_HEADS = TOK_WIDTH // HEAD_DIM
LEFT_CHUNKS = 8
BAND = (LEFT_CHUNKS + 1) * CHUNK
BAND_PAD = LEFT_CHUNKS * CHUNK
REL_CLIP = 128
CONV_WIDTH = 31
CONV_CH = TOK_WIDTH
A_IN = 3 * TOK_WIDTH + MEM_WIDTH
B_IN = 2 * CONV_CH + MEM_WIDTH
D_FF = 2816
EPS = 1e-6
NEG_INF = -1e30
ATTN_SCALE = HEAD_DIM ** -0.5

LANES = 128
PAIR = 2 * HEAD_DIM
TOK_TILE = 512
IN_SUB = 2
ATT_BLK = 4 * CHUNK
ATT_KEYS = ATT_BLK + BAND_PAD
ATT_ROWS = 32
ATT_SEQS = 4
LOG2E = 1.4426950408889634
COL_CHUNK = 256
FF_CHUNK = 256
CONV_HALO = 32
CONV_TILE = 512
CONV_ROWS = 64
SUBLANES = 8
VMEM_LIMIT = 56 * 1024 * 1024

BF16 = jnp.bfloat16
F32 = jnp.float32


def _const_spec(shape):
    return pl.BlockSpec(shape, lambda *_: (0,) * len(shape), pipeline_mode=pl.Buffered(1))


def _layer_spec(shape, layer):
    return pl.BlockSpec((None,) + shape, lambda *_: (layer,) + (0,) * len(shape), pipeline_mode=pl.Buffered(1))


def _rms(x, g):
    return x * lax.rsqrt(jnp.mean(x * x, axis=-1, keepdims=True) + EPS) * g


def _left_mask(shape):
    return lax.broadcasted_iota(jnp.int32, shape, len(shape) - 1) < HEAD_DIM


def _pair_head_norm(z, g):
    left = _left_mask(z.shape)
    sq = z * z
    ss_l = jnp.sum(jnp.where(left, sq, 0.0), axis=-1, keepdims=True)
    ss_r = jnp.sum(jnp.where(left, 0.0, sq), axis=-1, keepdims=True)
    r = jnp.where(left, lax.rsqrt(ss_l * (1.0 / HEAD_DIM) + EPS), lax.rsqrt(ss_r * (1.0 / HEAD_DIM) + EPS))
    return z * r * g


def _head_norm(z, g):
    parts = [_pair_head_norm(z[:, p * LANES:(p + 1) * LANES], g[:, p * LANES:(p + 1) * LANES])
             for p in range(z.shape[1] // LANES)]
    return jnp.concatenate(parts, axis=1) if len(parts) > 1 else parts[0]


def _stack_pair(q):
    left = _left_mask(q.shape)
    zero = jnp.zeros_like(q)
    return jnp.concatenate([jnp.where(left, q, zero), jnp.where(left, zero, q)], axis=0)


def _unstack_pair(o2):
    rows = o2.shape[0] // 2
    left = _left_mask((rows, LANES))
    return jnp.where(left, o2[:rows], o2[rows:])


def _nt_dot(a, b):
    return lax.dot_general(a, b, (((1,), (1,)), ((), ())), preferred_element_type=F32)


def _mem_kv_kernel(mem_ref, g_ref, w_ref, gk_ref, k_ref, v_ref):
    m = _rms(mem_ref[...], g_ref[...]).astype(BF16)
    kv = jnp.dot(m, w_ref[...].astype(BF16), preferred_element_type=F32)
    k_ref[...] = _head_norm(kv[:, :MEM_WIDTH], gk_ref[...]).astype(BF16)
    v_ref[...] = kv[:, MEM_WIDTH:].astype(BF16)


def _mem_kv(mem, g, w, gk, layer):
    B = mem.shape[0]
    return pl.pallas_call(
        _mem_kv_kernel,
        grid=(B,),
        in_specs=[
            pl.BlockSpec((None, MEM_TOKENS, D_MODEL), lambda b: (b, 0, 0)),
            _const_spec((1, D_MODEL)),
            _layer_spec((D_MODEL, 2 * MEM_WIDTH), layer),
            _const_spec((1, MEM_WIDTH)),
        ],
        out_specs=[
            pl.BlockSpec((None, MEM_TOKENS, MEM_WIDTH), lambda b: (b, 0, 0)),
            pl.BlockSpec((None, MEM_TOKENS, MEM_WIDTH), lambda b: (b, 0, 0)),
        ],
        out_shape=[jax.ShapeDtypeStruct((B, MEM_TOKENS, MEM_WIDTH), BF16)] * 2,
        compiler_params=pltpu.CompilerParams(dimension_semantics=("arbitrary",)),
        name="mem_kv",
    )(mem, g, w, gk)


def _in_a_kernel(x_ref, g1_ref, w_ref, gq_ref, gk_ref, gm_ref, o_ref):
    n_tok = TOK_WIDTH // COL_CHUNK
    for sub in range(IN_SUB):
        rows = slice(sub * TOK_TILE, (sub + 1) * TOK_TILE)
        h = _rms(x_ref[rows, :], g1_ref[...]).astype(BF16)
        for j in range(A_IN // COL_CHUNK):
            cols = slice(j * COL_CHUNK, (j + 1) * COL_CHUNK)
            z = jnp.dot(h, w_ref[:, cols].astype(BF16), preferred_element_type=F32)
            if j < n_tok:
                z = _head_norm(z, gq_ref[...])
            elif j < 2 * n_tok:
                z = _head_norm(z, gk_ref[...])
            elif j >= 3 * n_tok:
                z = _head_norm(z, gm_ref[...])
            o_ref[rows, cols] = z.astype(BF16)


def _in_proj_a(x2, g1, w, gq, gk, gm):
    T = x2.shape[0]
    return pl.pallas_call(
        _in_a_kernel,
        grid=(T // (IN_SUB * TOK_TILE),),
        in_specs=[
            pl.BlockSpec((IN_SUB * TOK_TILE, D_MODEL), lambda i: (i, 0)),
            _const_spec((1, D_MODEL)),
            _const_spec((D_MODEL, A_IN)),
            _const_spec((1, COL_CHUNK)),
            _const_spec((1, COL_CHUNK)),
            _const_spec((1, COL_CHUNK)),
        ],
        out_specs=pl.BlockSpec((IN_SUB * TOK_TILE, A_IN), lambda i: (i, 0)),
        out_shape=jax.ShapeDtypeStruct((T, A_IN), BF16),
        compiler_params=pltpu.CompilerParams(dimension_semantics=("arbitrary",), vmem_limit_bytes=VMEM_LIMIT),
        name="in_proj_a",
    )(x2, g1, w, gq, gk, gm)


def _in_b_kernel(x_ref, g1_ref, w_ref, b_ref, gm_ref, h_ref, qm_ref):
    n_ch = CONV_CH // COL_CHUNK
    cm = slice(2 * CONV_CH, B_IN)
    for sub in range(IN_SUB):
        rows = slice(sub * TOK_TILE, (sub + 1) * TOK_TILE)
        h = _rms(x_ref[rows, :], g1_ref[...]).astype(BF16)
        for j in range(n_ch):
            ca = slice(j * COL_CHUNK, (j + 1) * COL_CHUNK)
            cg = slice(CONV_CH + j * COL_CHUNK, CONV_CH + (j + 1) * COL_CHUNK)
            a = jnp.dot(h, w_ref[:, ca].astype(BF16), preferred_element_type=F32) + b_ref[:, ca]
            gate = jnp.dot(h, w_ref[:, cg].astype(BF16), preferred_element_type=F32) + b_ref[:, cg]
            h_ref[rows, ca] = a * jax.nn.sigmoid(gate)
        zm = jnp.dot(h, w_ref[:, cm].astype(BF16), preferred_element_type=F32) + b_ref[:, cm]
        qm_ref[rows, :] = _head_norm(zm, gm_ref[...]).astype(BF16)


def _in_proj_b(x2, g1, w, b, gm):
    T = x2.shape[0]
    return pl.pallas_call(
        _in_b_kernel,
        grid=(T // (IN_SUB * TOK_TILE),),
        in_specs=[
            pl.BlockSpec((IN_SUB * TOK_TILE, D_MODEL), lambda i: (i, 0)),
            _const_spec((1, D_MODEL)),
            _const_spec((D_MODEL, B_IN)),
            _const_spec((1, B_IN)),
            _const_spec((1, MEM_WIDTH)),
        ],
        out_specs=[
            pl.BlockSpec((IN_SUB * TOK_TILE, CONV_CH), lambda i: (i, 0)),
            pl.BlockSpec((IN_SUB * TOK_TILE, MEM_WIDTH), lambda i: (i, 0)),
        ],
        out_shape=[jax.ShapeDtypeStruct((T, CONV_CH), F32), jax.ShapeDtypeStruct((T, MEM_WIDTH), BF16)],
        compiler_params=pltpu.CompilerParams(dimension_semantics=("arbitrary",), vmem_limit_bytes=VMEM_LIMIT),
        name="in_proj_b",
    )(x2, g1, w, b, gm)


def _attn_kernel(q_ref, k_ref, v_ref, bias_ref, o_ref, s_ref, p_ref, va_ref):
    seq = q_ref.shape[1]
    lane = lax.broadcasted_iota(jnp.int32, (seq, LANES), 1)
    for sq in range(q_ref.shape[0]):
        va_ref[sq, :, 0:LANES] = v_ref[sq]
        va_ref[sq, :, LANES:2 * LANES] = jnp.where(lane == 0, 1.0, 0.0).astype(BF16)
    for sq in range(q_ref.shape[0]):
        for i in range(seq // ATT_BLK):
            q0 = i * ATT_BLK
            k_lo = max(q0 - BAND_PAD, 0)
            k_hi = q0 + ATT_BLK
            n_keys = k_hi - k_lo
            b_lo = ATT_KEYS - n_keys
            q2 = _stack_pair(q_ref[sq, q0:q0 + ATT_BLK, :])
            s_ref[:, 0:n_keys] = _nt_dot(q2, k_ref[sq, k_lo:k_hi, :])
            for r in range(0, 2 * ATT_BLK, ATT_ROWS):
                s = s_ref[r:r + ATT_ROWS, 0:n_keys] + bias_ref[r:r + ATT_ROWS, b_lo:ATT_KEYS]
                m = jnp.max(s, axis=-1, keepdims=True)
                p_ref[r:r + ATT_ROWS, 0:n_keys] = jnp.exp2(s - m).astype(BF16)
            o2 = jnp.dot(p_ref[:, 0:n_keys], va_ref[sq, k_lo:k_hi, :], preferred_element_type=F32)
            o2 = o2[:, 0:LANES] * (1.0 / o2[:, LANES:LANES + 1])
            o_ref[sq, q0:q0 + ATT_BLK, :] = _unstack_pair(o2).astype(BF16)


def _band_attention(z3, bias):
    B, S, _ = z3.shape
    n_pairs = TOK_WIDTH // LANES
    blk = (ATT_SEQS, S, LANES)
    return pl.pallas_call(
        _attn_kernel,
        grid=(n_pairs, B // ATT_SEQS),
        in_specs=[
            pl.BlockSpec(blk, lambda p, b: (b, 0, p)),
            pl.BlockSpec(blk, lambda p, b: (b, 0, n_pairs + p)),
            pl.BlockSpec(blk, lambda p, b: (b, 0, 2 * n_pairs + p)),
            pl.BlockSpec((None, 2 * ATT_BLK, ATT_KEYS), lambda p, b: (p, 0, 0)),
        ],
        out_specs=pl.BlockSpec(blk, lambda p, b: (b, 0, p)),
        out_shape=jax.ShapeDtypeStruct((B, S, TOK_WIDTH), BF16),
        scratch_shapes=[
            pltpu.VMEM((2 * ATT_BLK, ATT_KEYS), F32),
            pltpu.VMEM((2 * ATT_BLK, ATT_KEYS), BF16),
            pltpu.VMEM((ATT_SEQS, S, 2 * LANES), BF16),
        ],
        compiler_params=pltpu.CompilerParams(dimension_semantics=("arbitrary",) * 2),
        name="band_attn",
    )(z3, z3, z3, bias)


def _band_bias(rel_bias):
    n_rel = rel_bias.shape[1]
    ext = jnp.concatenate(
        [jnp.broadcast_to(rel_bias[:, n_rel - 1:], (ATT_HEADS, BAND - REL_CLIP)), rel_bias[:, n_rel - 2::-1]], axis=1)
    chunk = jnp.stack([ext[:, CHUNK - 1 - qi:CHUNK - 1 - qi + BAND] for qi in range(CHUNK)], axis=1)
    n_cq = ATT_BLK // CHUNK
    blk = jnp.stack(
        [jnp.pad(chunk, ((0, 0), (0, 0), (cq * CHUNK, ATT_KEYS - BAND - cq * CHUNK)), constant_values=NEG_INF)
         for cq in range(n_cq)], axis=1)
    return (blk * LOG2E).reshape(ATT_HEADS // 2, 2 * ATT_BLK, ATT_KEYS).astype(F32)


def _conv_kernel(halo_ref, h_ref, w_ref, cb_ref, lg_ref, lb_ref, o_ref, hs_ref, y_ref):
    first = pl.program_id(1) == 0
    hs_ref[0, 0:CONV_HALO, :] = jnp.where(first, 0.0, halo_ref[...])
    hs_ref[0, CONV_HALO:CONV_HALO + CONV_TILE, :] = h_ref[...]
    n_shift = hs_ref.shape[1] - SUBLANES
    for b in range(1, SUBLANES):
        hs_ref[b, 0:n_shift, :] = hs_ref[0, b:b + n_shift, :]
    base = CONV_HALO - (CONV_WIDTH - 1)

    n_blocks = CONV_TILE // CONV_ROWS

    def conv_rows(rb):
        r = rb * CONV_ROWS if isinstance(rb, int) else pl.multiple_of(rb * CONV_ROWS, CONV_ROWS)
        for cb in range(CONV_CH // LANES):
            cols = slice(cb * LANES, (cb + 1) * LANES)
            n_acc = CONV_ROWS // SUBLANES
            acc = [jnp.zeros((SUBLANES, LANES), F32)] * n_acc
            for b in range(SUBLANES):
                taps = [(j, (base + j) // SUBLANES) for j in range(CONV_WIDTH) if (base + j) % SUBLANES == b]
                w_of = {a: w_ref[j, :, cols] for j, a in taps}
                for g in range(min(w_of), n_acc + max(w_of)):
                    xg = hs_ref[b, pl.ds(r + g * SUBLANES, SUBLANES), cols]
                    for a, wj in w_of.items():
                        if 0 <= g - a < n_acc:
                            acc[g - a] = acc[g - a] + xg * wj
            for k in range(CONV_ROWS // SUBLANES):
                y_ref[rb % 2, k * SUBLANES:(k + 1) * SUBLANES, cols] = acc[k] + cb_ref[:, cols]

    def norm_rows(rb):
        r = rb * CONV_ROWS if isinstance(rb, int) else pl.multiple_of(rb * CONV_ROWS, CONV_ROWS)
        y = y_ref[rb % 2]
        mu = jnp.mean(y, axis=-1, keepdims=True)
        yc = y - mu
        yn = yc * lax.rsqrt(jnp.mean(yc * yc, axis=-1, keepdims=True) + EPS) * lg_ref[...] + lb_ref[...]
        o_ref[pl.ds(r, CONV_ROWS), :] = (yn * jax.nn.sigmoid(yn)).astype(BF16)

    def step(rb, carry):
        norm_rows(rb - 1)
        conv_rows(rb)
        return carry

    conv_rows(0)
    lax.fori_loop(1, n_blocks, step, 0)
    norm_rows(n_blocks - 1)


def _conformer_conv(h3, w, cb, lg, lb):
    B, S, _ = h3.shape
    per = CONV_TILE // CONV_HALO
    return pl.pallas_call(
        _conv_kernel,
        grid=(B, S // CONV_TILE),
        in_specs=[
            pl.BlockSpec((None, CONV_HALO, CONV_CH), lambda b, s: (b, jnp.maximum(s * per - 1, 0), 0)),
            pl.BlockSpec((None, CONV_TILE, CONV_CH), lambda b, s: (b, s, 0)),
            _const_spec((CONV_WIDTH, SUBLANES, CONV_CH)),
            _const_spec((1, CONV_CH)),
            _const_spec((1, CONV_CH)),
            _const_spec((1, CONV_CH)),
        ],
        out_specs=pl.BlockSpec((None, CONV_TILE, CONV_CH), lambda b, s: (b, s, 0)),
        out_shape=jax.ShapeDtypeStruct((B, S, CONV_CH), BF16),
        scratch_shapes=[pltpu.VMEM((SUBLANES, CONV_HALO + CONV_TILE, CONV_CH), F32),
                        pltpu.VMEM((2, CONV_ROWS, CONV_CH), F32)],
        compiler_params=pltpu.CompilerParams(dimension_semantics=("arbitrary",) * 2, vmem_limit_bytes=VMEM_LIMIT),
        name="conformer_conv",
    )(h3, h3, w, cb, lg, lb)


def _post_head(x_ref, tok_ref, qm_ref, km_ref, vm_ref, wo_ref, g2_ref, o_ref, memo_ref, h2_ref):
    for p in range(MEM_WIDTH // LANES):
        cols = slice(p * LANES, (p + 1) * LANES)
        q2 = _stack_pair(qm_ref[:, cols])
        s = _nt_dot(q2, km_ref[:, cols])
        m = jnp.max(s, axis=-1, keepdims=True)
        e = jnp.exp2(s - m)
        l = jnp.sum(e, axis=-1, keepdims=True)
        o2 = jnp.dot(e.astype(BF16), vm_ref[:, cols], preferred_element_type=F32) * (1.0 / l)
        memo_ref[:, cols] = _unstack_pair(o2).astype(BF16)
    y = jnp.dot(tok_ref[...], wo_ref[0:TOK_WIDTH, :].astype(BF16), preferred_element_type=F32)
    y = y + jnp.dot(memo_ref[...], wo_ref[TOK_WIDTH:D_MODEL, :].astype(BF16), preferred_element_type=F32)
    x1 = x_ref[...] + y
    o_ref[...] = x1
    h2_ref[...] = _rms(x1, g2_ref[...]).astype(BF16)


def _ffn_chunk(h2_ref, wg, wu, wd, o_ref):
    h2 = h2_ref[...]
    g = jnp.dot(h2, wg, preferred_element_type=F32)
    u = jnp.dot(h2, wu, preferred_element_type=F32)
    a = (g * jax.nn.sigmoid(g) * u).astype(BF16)
    o_ref[...] += jnp.dot(a, wd, preferred_element_type=F32)


def _post_a_kernel(x_ref, tok_ref, qm_ref, km_ref, vm_ref, wo_ref, g2_ref, wg_ref, wu_ref, wd_ref,
                   o_ref, memo_ref, h2_ref):
    _post_head(x_ref, tok_ref, qm_ref, km_ref, vm_ref, wo_ref, g2_ref, o_ref, memo_ref, h2_ref)
    for c in range(D_FF // FF_CHUNK):
        cs = slice(c * FF_CHUNK, (c + 1) * FF_CHUNK)
        _ffn_chunk(h2_ref, wg_ref[:, cs].astype(BF16), wu_ref[:, cs].astype(BF16), wd_ref[cs, :].astype(BF16), o_ref)


def _post_specs(T, n_seq, qm_block, layer):
    tiles_per_seq = (T // n_seq) // TOK_TILE
    in_specs = [
        pl.BlockSpec((TOK_TILE, D_MODEL), lambda i: (i, 0)),
        pl.BlockSpec((TOK_TILE, MEM_WIDTH), lambda i: (i, qm_block)),
        pl.BlockSpec((None, MEM_TOKENS, MEM_WIDTH), lambda i: (i // tiles_per_seq, 0, 0)),
        pl.BlockSpec((None, MEM_TOKENS, MEM_WIDTH), lambda i: (i // tiles_per_seq, 0, 0)),
        _layer_spec((D_MODEL, D_MODEL), layer),
        _const_spec((1, D_MODEL)),
        _layer_spec((D_MODEL, D_FF), layer),
        _layer_spec((D_MODEL, D_FF), layer),
        _layer_spec((D_FF, D_MODEL), layer),
    ]
    scratch = [pltpu.VMEM((TOK_TILE, MEM_WIDTH), BF16), pltpu.VMEM((TOK_TILE, D_MODEL), BF16)]
    return tiles_per_seq, in_specs, scratch


def _post_a(x2, tok2, qm2, km, vm, wo, g2, wg, wu, wd, qm_block, layer):
    T = x2.shape[0]
    _, in_specs, scratch = _post_specs(T, km.shape[0], qm_block, layer)
    in_specs.insert(1, pl.BlockSpec((TOK_TILE, TOK_WIDTH), lambda i: (i, 0)))
    return pl.pallas_call(
        _post_a_kernel,
        grid=(T // TOK_TILE,),
        in_specs=in_specs,
        out_specs=pl.BlockSpec((TOK_TILE, D_MODEL), lambda i: (i, 0)),
        out_shape=jax.ShapeDtypeStruct((T, D_MODEL), F32),
        scratch_shapes=scratch,
        compiler_params=pltpu.CompilerParams(dimension_semantics=("arbitrary",), vmem_limit_bytes=VMEM_LIMIT),
        name="post_a",
    )(x2, tok2, qm2, km, vm, wo, g2, wg, wu, wd)


def _row(v):
    return v.reshape(1, -1).astype(F32)


def _tile_heads(g, width, scale=1.0):
    return _row(jnp.tile(g * scale, width // HEAD_DIM))


def kernel(x, mem, norm1_g, mem_norm_g, a_w_in, a_q_g, a_k_g, a_rel_bias, b_w_in, b_b_in, b_conv_w, b_conv_b,
           b_ln_g, b_ln_b, mq_g, mk_g, w_mem_kv, w_out, norm2_g, w_gate, w_up, w_down):
    B, S, D = x.shape
    T = B * S
    depth = norm1_g.shape[0]
    x2 = x.reshape(T, D)
    for i in range(depth):
        j = i // 2
        km, vm = _mem_kv(mem, _row(mem_norm_g[i]), w_mem_kv, _tile_heads(mk_g[i], MEM_WIDTH), i)
        gm = _tile_heads(mq_g[i], MEM_WIDTH, ATTN_SCALE * LOG2E)
        ff = (w_out, _row(norm2_g[i]), w_gate, w_up, w_down)
        if i % 2 == 0:
            z = _in_proj_a(x2, _row(norm1_g[i]), a_w_in[j],
                           _tile_heads(a_q_g[j], COL_CHUNK, ATTN_SCALE * LOG2E), _tile_heads(a_k_g[j], COL_CHUNK), gm)
            tok = _band_attention(z.reshape(B, S, A_IN), _band_bias(a_rel_bias[j])).reshape(T, TOK_WIDTH)
            x2 = _post_a(x2, tok, z, km, vm, *ff, (3 * TOK_WIDTH) // MEM_WIDTH, i)
        else:
            h, qm2 = _in_proj_b(x2, _row(norm1_g[i]), b_w_in[j], _row(b_b_in[j]), gm)
            conv_w = jnp.broadcast_to(b_conv_w[j].astype(F32)[:, None, :], (CONV_WIDTH, SUBLANES, CONV_CH))
            tok = _conformer_conv(h.reshape(B, S, CONV_CH), conv_w, _row(b_conv_b[j]),
                                  _row(b_ln_g[j]), _row(b_ln_b[j])).reshape(T, CONV_CH)
            x2 = _post_a(x2, tok, qm2, km, vm, *ff, 0, i)
    return x2.reshape(B, S, D)
```

```python
import jax
import jax.numpy as jnp
from jax import lax
from jax.experimental import pallas as pl
from jax.experimental.pallas import tpu as pltpu

D_MODEL = 1024
CHUNK = 64
HEAD_DIM = 64
MEM_TOKENS = 256
MEM_HEADS = 4
MEM_WIDTH = MEM_HEADS * HEAD_DIM
TOK_WIDTH = D_MODEL - MEM_WIDTH
ATT_HEADS = TOK_WIDTH // HEAD_DIM
LEFT_CHUNKS = 8
BAND = (LEFT_CHUNKS + 1) * CHUNK
BAND_PAD = LEFT_CHUNKS * CHUNK
REL_CLIP = 128
CONV_WIDTH = 31
CONV_CH = TOK_WIDTH
A_IN = 3 * TOK_WIDTH + MEM_WIDTH
B_IN = 2 * CONV_CH + MEM_WIDTH
D_FF = 2816
EPS = 1e-6
NEG_INF = -1e30
ATTN_SCALE = HEAD_DIM ** -0.5

LANES = 128
PAIR = 2 * HEAD_DIM
TOK_TILE = 512
IN_SUB = 4
MEM_SEQS = 4
ATT_BLK = 4 * CHUNK
ATT_KEYS = ATT_BLK + BAND_PAD
ATT_ROWS = 32
ATT_SEQS = 4
LOG2E = 1.4426950408889634
COL_CHUNK = 256
FF_CHUNK = 256
CONV_HALO = 32
CONV_TILE = 512
CONV_ROWS = 64
SUBLANES = 8
VMEM_LIMIT = 56 * 1024 * 1024

BF16 = jnp.bfloat16
F32 = jnp.float32


def _const_spec(shape):
    return pl.BlockSpec(shape, lambda *_: (0,) * len(shape), pipeline_mode=pl.Buffered(1))


def _layer_spec(shape, layer):
    return pl.BlockSpec((None,) + shape, lambda *_: (layer,) + (0,) * len(shape), pipeline_mode=pl.Buffered(1))


def _rms(x, g):
    return x * lax.rsqrt(jnp.mean(x * x, axis=-1, keepdims=True) + EPS) * g


def _left_mask(shape):
    return lax.broadcasted_iota(jnp.int32, shape, len(shape) - 1) < HEAD_DIM


def _pair_head_norm(z, g):
    left = _left_mask(z.shape)
    sq = z * z
    ss_l = jnp.sum(jnp.where(left, sq, 0.0), axis=-1, keepdims=True)
    ss_r = jnp.sum(jnp.where(left, 0.0, sq), axis=-1, keepdims=True)
    r = jnp.where(left, lax.rsqrt(ss_l * (1.0 / HEAD_DIM) + EPS), lax.rsqrt(ss_r * (1.0 / HEAD_DIM) + EPS))
    return z * r * g


def _head_norm(z, g):
    parts = [_pair_head_norm(z[:, p * LANES:(p + 1) * LANES], g[:, p * LANES:(p + 1) * LANES])
             for p in range(z.shape[1] // LANES)]
    return jnp.concatenate(parts, axis=1) if len(parts) > 1 else parts[0]


def _stack_pair(q):
    left = _left_mask(q.shape)
    zero = jnp.zeros_like(q)
    return jnp.concatenate([jnp.where(left, q, zero), jnp.where(left, zero, q)], axis=0)


def _unstack_pair(o2):
    rows = o2.shape[0] // 2
    left = _left_mask((rows, LANES))
    return jnp.where(left, o2[:rows], o2[rows:])


def _nt_dot(a, b):
    return lax.dot_general(a, b, (((1,), (1,)), ((), ())), preferred_element_type=F32)


def _mem_kv_kernel(mem_ref, g_ref, w_ref, gk_ref, k_ref, v_ref):
    m = _rms(mem_ref[...], g_ref[...]).astype(BF16)
    kv = jnp.dot(m, w_ref[...].astype(BF16), preferred_element_type=F32)
    k_ref[...] = _head_norm(kv[:, :MEM_WIDTH], gk_ref[...]).astype(BF16)
    v_ref[...] = kv[:, MEM_WIDTH:].astype(BF16)


def _mem_kv(mem, g, w, gk, layer):
    B = mem.shape[0]
    rows = MEM_SEQS * MEM_TOKENS
    k, v = pl.pallas_call(
        _mem_kv_kernel,
        grid=(B // MEM_SEQS,),
        in_specs=[
            pl.BlockSpec((rows, D_MODEL), lambda b: (b, 0)),
            _const_spec((1, D_MODEL)),
            _layer_spec((D_MODEL, 2 * MEM_WIDTH), layer),
            _const_spec((1, MEM_WIDTH)),
        ],
        out_specs=[
            pl.BlockSpec((rows, MEM_WIDTH), lambda b: (b, 0)),
            pl.BlockSpec((rows, MEM_WIDTH), lambda b: (b, 0)),
        ],
        out_shape=[jax.ShapeDtypeStruct((B * MEM_TOKENS, MEM_WIDTH), BF16)] * 2,
        compiler_params=pltpu.CompilerParams(dimension_semantics=("arbitrary",)),
        name="mem_kv",
    )(mem.reshape(B * MEM_TOKENS, D_MODEL), g, w, gk)
    return k.reshape(B, MEM_TOKENS, MEM_WIDTH), v.reshape(B, MEM_TOKENS, MEM_WIDTH)


def _in_a_kernel(x_ref, g1_ref, w_ref, gq_ref, gk_ref, gm_ref, o_ref):
    n_tok = TOK_WIDTH // COL_CHUNK
    for sub in range(IN_SUB):
        rows = slice(sub * TOK_TILE, (sub + 1) * TOK_TILE)
        h = _rms(x_ref[rows, :], g1_ref[...]).astype(BF16)
        for j in range(A_IN // COL_CHUNK):
            cols = slice(j * COL_CHUNK, (j + 1) * COL_CHUNK)
            z = jnp.dot(h, w_ref[:, cols].astype(BF16), preferred_element_type=F32)
            if j < n_tok:
                z = _head_norm(z, gq_ref[...])
            elif j < 2 * n_tok:
                z = _head_norm(z, gk_ref[...])
            elif j >= 3 * n_tok:
                z = _head_norm(z, gm_ref[...])
            o_ref[rows, cols] = z.astype(BF16)


def _in_proj_a(x2, g1, w, gq, gk, gm):
    T = x2.shape[0]
    return pl.pallas_call(
        _in_a_kernel,
        grid=(T // (IN_SUB * TOK_TILE),),
        in_specs=[
            pl.BlockSpec((IN_SUB * TOK_TILE, D_MODEL), lambda i: (i, 0)),
            _const_spec((1, D_MODEL)),
            _const_spec((D_MODEL, A_IN)),
            _const_spec((1, COL_CHUNK)),
            _const_spec((1, COL_CHUNK)),
            _const_spec((1, COL_CHUNK)),
        ],
        out_specs=pl.BlockSpec((IN_SUB * TOK_TILE, A_IN), lambda i: (i, 0)),
        out_shape=jax.ShapeDtypeStruct((T, A_IN), BF16),
        compiler_params=pltpu.CompilerParams(dimension_semantics=("arbitrary",), vmem_limit_bytes=VMEM_LIMIT),
        name="in_proj_a",
    )(x2, g1, w, gq, gk, gm)


def _in_b_kernel(x_ref, g1_ref, w_ref, b_ref, gm_ref, h_ref, qm_ref):
    n_ch = CONV_CH // COL_CHUNK
    cm = slice(2 * CONV_CH, B_IN)
    for sub in range(IN_SUB):
        rows = slice(sub * TOK_TILE, (sub + 1) * TOK_TILE)
        h = _rms(x_ref[rows, :], g1_ref[...]).astype(BF16)
        for j in range(n_ch):
            ca = slice(j * COL_CHUNK, (j + 1) * COL_CHUNK)
            cg = slice(CONV_CH + j * COL_CHUNK, CONV_CH + (j + 1) * COL_CHUNK)
            a = jnp.dot(h, w_ref[:, ca].astype(BF16), preferred_element_type=F32) + b_ref[:, ca]
            gate = jnp.dot(h, w_ref[:, cg].astype(BF16), preferred_element_type=F32) + b_ref[:, cg]
            h_ref[rows, ca] = a * jax.nn.sigmoid(gate)
        zm = jnp.dot(h, w_ref[:, cm].astype(BF16), preferred_element_type=F32) + b_ref[:, cm]
        qm_ref[rows, :] = _head_norm(zm, gm_ref[...]).astype(BF16)


def _in_proj_b(x2, g1, w, b, gm):
    T = x2.shape[0]
    return pl.pallas_call(
        _in_b_kernel,
        grid=(T // (IN_SUB * TOK_TILE),),
        in_specs=[
            pl.BlockSpec((IN_SUB * TOK_TILE, D_MODEL), lambda i: (i, 0)),
            _const_spec((1, D_MODEL)),
            _const_spec((D_MODEL, B_IN)),
            _const_spec((1, B_IN)),
            _const_spec((1, MEM_WIDTH)),
        ],
        out_specs=[
            pl.BlockSpec((IN_SUB * TOK_TILE, CONV_CH), lambda i: (i, 0)),
            pl.BlockSpec((IN_SUB * TOK_TILE, MEM_WIDTH), lambda i: (i, 0)),
        ],
        out_shape=[jax.ShapeDtypeStruct((T, CONV_CH), F32), jax.ShapeDtypeStruct((T, MEM_WIDTH), BF16)],
        compiler_params=pltpu.CompilerParams(dimension_semantics=("arbitrary",), vmem_limit_bytes=VMEM_LIMIT),
        name="in_proj_b",
    )(x2, g1, w, b, gm)


def _attn_kernel(q_ref, k_ref, v_ref, bias_ref, o_ref, s_ref, p_ref, va_ref):
    seq = q_ref.shape[1]
    lane = lax.broadcasted_iota(jnp.int32, (seq, LANES), 1)
    for sq in range(q_ref.shape[0]):
        va_ref[sq, :, 0:LANES] = v_ref[sq]
        va_ref[sq, :, LANES:2 * LANES] = jnp.where(lane == 0, 1.0, 0.0).astype(BF16)
    for sq in range(q_ref.shape[0]):
        for i in range(seq // ATT_BLK):
            q0 = i * ATT_BLK
            k_lo = max(q0 - BAND_PAD, 0)
            k_hi = q0 + ATT_BLK
            n_keys = k_hi - k_lo
            b_lo = ATT_KEYS - n_keys
            q2 = _stack_pair(q_ref[sq, q0:q0 + ATT_BLK, :])
            s_ref[:, 0:n_keys] = _nt_dot(q2, k_ref[sq, k_lo:k_hi, :])
            for r in range(0, 2 * ATT_BLK, ATT_ROWS):
                s = s_ref[r:r + ATT_ROWS, 0:n_keys] + bias_ref[r:r + ATT_ROWS, b_lo:ATT_KEYS]
                m = jnp.max(s, axis=-1, keepdims=True)
                p_ref[r:r + ATT_ROWS, 0:n_keys] = jnp.exp2(s - m).astype(BF16)
            o2 = jnp.dot(p_ref[:, 0:n_keys], va_ref[sq, k_lo:k_hi, :], preferred_element_type=F32)
            o2 = o2[:, 0:LANES] * (1.0 / o2[:, LANES:LANES + 1])
            o_ref[sq, q0:q0 + ATT_BLK, :] = _unstack_pair(o2).astype(BF16)


def _band_attention(z3, bias):
    B, S, _ = z3.shape
    n_pairs = TOK_WIDTH // LANES
    blk = (ATT_SEQS, S, LANES)
    return pl.pallas_call(
        _attn_kernel,
        grid=(n_pairs, B // ATT_SEQS),
        in_specs=[
            pl.BlockSpec(blk, lambda p, b: (b, 0, p)),
            pl.BlockSpec(blk, lambda p, b: (b, 0, n_pairs + p)),
            pl.BlockSpec(blk, lambda p, b: (b, 0, 2 * n_pairs + p)),
            pl.BlockSpec((None, 2 * ATT_BLK, ATT_KEYS), lambda p, b: (p, 0, 0)),
        ],
        out_specs=pl.BlockSpec(blk, lambda p, b: (b, 0, p)),
        out_shape=jax.ShapeDtypeStruct((B, S, TOK_WIDTH), BF16),
        scratch_shapes=[
            pltpu.VMEM((2 * ATT_BLK, ATT_KEYS), F32),
            pltpu.VMEM((2 * ATT_BLK, ATT_KEYS), BF16),
            pltpu.VMEM((ATT_SEQS, S, 2 * LANES), BF16),
        ],
        compiler_params=pltpu.CompilerParams(dimension_semantics=("arbitrary",) * 2),
        name="band_attn",
    )(z3, z3, z3, bias)


def _band_bias(rel_bias):
    n_rel = rel_bias.shape[1]
    ext = jnp.concatenate(
        [jnp.broadcast_to(rel_bias[:, n_rel - 1:], (ATT_HEADS, BAND - REL_CLIP)), rel_bias[:, n_rel - 2::-1]], axis=1)
    chunk = jnp.stack([ext[:, CHUNK - 1 - qi:CHUNK - 1 - qi + BAND] for qi in range(CHUNK)], axis=1)
    n_cq = ATT_BLK // CHUNK
    blk = jnp.stack(
        [jnp.pad(chunk, ((0, 0), (0, 0), (cq * CHUNK, ATT_KEYS - BAND - cq * CHUNK)), constant_values=NEG_INF)
         for cq in range(n_cq)], axis=1)
    return (blk * LOG2E).reshape(ATT_HEADS // 2, 2 * ATT_BLK, ATT_KEYS).astype(F32)


def _conv_kernel(halo_ref, h_ref, w_ref, cb_ref, lg_ref, lb_ref, o_ref, hs_ref, y_ref):
    first = pl.program_id(1) == 0
    hs_ref[0, 0:CONV_HALO, :] = jnp.where(first, 0.0, halo_ref[...])
    hs_ref[0, CONV_HALO:CONV_HALO + CONV_TILE, :] = h_ref[...]
    n_shift = hs_ref.shape[1] - SUBLANES
    for b in range(1, SUBLANES):
        hs_ref[b, 0:n_shift, :] = hs_ref[0, b:b + n_shift, :]
    base = CONV_HALO - (CONV_WIDTH - 1)

    n_blocks = CONV_TILE // CONV_ROWS

    def conv_rows(rb):
        r = rb * CONV_ROWS if isinstance(rb, int) else pl.multiple_of(rb * CONV_ROWS, CONV_ROWS)
        for cb in range(CONV_CH // LANES):
            cols = slice(cb * LANES, (cb + 1) * LANES)
            n_acc = CONV_ROWS // SUBLANES
            acc = [jnp.zeros((SUBLANES, LANES), F32)] * n_acc
            for b in range(SUBLANES):
                taps = [(j, (base + j) // SUBLANES) for j in range(CONV_WIDTH) if (base + j) % SUBLANES == b]
                w_of = {a: w_ref[j, :, cols] for j, a in taps}
                for g in range(min(w_of), n_acc + max(w_of)):
                    xg = hs_ref[b, pl.ds(r + g * SUBLANES, SUBLANES), cols]
                    for a, wj in w_of.items():
                        if 0 <= g - a < n_acc:
                            acc[g - a] = acc[g - a] + xg * wj
            for k in range(CONV_ROWS // SUBLANES):
                y_ref[rb % 2, k * SUBLANES:(k + 1) * SUBLANES, cols] = acc[k] + cb_ref[:, cols]

    def norm_rows(rb):
        r = rb * CONV_ROWS if isinstance(rb, int) else pl.multiple_of(rb * CONV_ROWS, CONV_ROWS)
        y = y_ref[rb % 2]
        mu = jnp.mean(y, axis=-1, keepdims=True)
        yc = y - mu
        yn = yc * lax.rsqrt(jnp.mean(yc * yc, axis=-1, keepdims=True) + EPS) * lg_ref[...] + lb_ref[...]
        o_ref[pl.ds(r, CONV_ROWS), :] = (yn * jax.nn.sigmoid(yn)).astype(BF16)

    def step(rb, carry):
        norm_rows(rb - 1)
        conv_rows(rb)
        return carry

    conv_rows(0)
    lax.fori_loop(1, n_blocks, step, 0)
    norm_rows(n_blocks - 1)


def _conformer_conv(h3, w, cb, lg, lb):
    B, S, _ = h3.shape
    per = CONV_TILE // CONV_HALO
    return pl.pallas_call(
        _conv_kernel,
        grid=(B, S // CONV_TILE),
        in_specs=[
            pl.BlockSpec((None, CONV_HALO, CONV_CH), lambda b, s: (b, jnp.maximum(s * per - 1, 0), 0)),
            pl.BlockSpec((None, CONV_TILE, CONV_CH), lambda b, s: (b, s, 0)),
            _const_spec((CONV_WIDTH, SUBLANES, CONV_CH)),
            _const_spec((1, CONV_CH)),
            _const_spec((1, CONV_CH)),
            _const_spec((1, CONV_CH)),
        ],
        out_specs=pl.BlockSpec((None, CONV_TILE, CONV_CH), lambda b, s: (b, s, 0)),
        out_shape=jax.ShapeDtypeStruct((B, S, CONV_CH), BF16),
        scratch_shapes=[pltpu.VMEM((SUBLANES, CONV_HALO + CONV_TILE, CONV_CH), F32),
                        pltpu.VMEM((2, CONV_ROWS, CONV_CH), F32)],
        compiler_params=pltpu.CompilerParams(dimension_semantics=("arbitrary",) * 2, vmem_limit_bytes=VMEM_LIMIT),
        name="conformer_conv",
    )(h3, h3, w, cb, lg, lb)


def _post_head(x_ref, tok_ref, qm_ref, km_ref, vm_ref, wo_ref, g2_ref, o_ref, memo_ref, h2_ref):
    for p in range(MEM_WIDTH // LANES):
        cols = slice(p * LANES, (p + 1) * LANES)
        q2 = _stack_pair(qm_ref[:, cols])
        s = _nt_dot(q2, km_ref[:, cols])
        m = jnp.max(s, axis=-1, keepdims=True)
        e = jnp.exp2(s - m)
        l = jnp.sum(e, axis=-1, keepdims=True)
        o2 = jnp.dot(e.astype(BF16), vm_ref[:, cols], preferred_element_type=F32) * (1.0 / l)
        memo_ref[:, cols] = _unstack_pair(o2).astype(BF16)
    y = jnp.dot(tok_ref[...], wo_ref[0:TOK_WIDTH, :].astype(BF16), preferred_element_type=F32)
    y = y + jnp.dot(memo_ref[...], wo_ref[TOK_WIDTH:D_MODEL, :].astype(BF16), preferred_element_type=F32)
    x1 = x_ref[...] + y
    o_ref[...] = x1
    h2_ref[...] = _rms(x1, g2_ref[...]).astype(BF16)


def _ffn_chunk(h2_ref, wg, wu, wd, o_ref):
    h2 = h2_ref[...]
    g = jnp.dot(h2, wg, preferred_element_type=F32)
    u = jnp.dot(h2, wu, preferred_element_type=F32)
    a = (g * jax.nn.sigmoid(g) * u).astype(BF16)
    o_ref[...] += jnp.dot(a, wd, preferred_element_type=F32)


def _post_a_kernel(x_ref, tok_ref, qm_ref, km_ref, vm_ref, wo_ref, g2_ref, wg_ref, wu_ref, wd_ref,
                   o_ref, memo_ref, h2_ref):
    _post_head(x_ref, tok_ref, qm_ref, km_ref, vm_ref, wo_ref, g2_ref, o_ref, memo_ref, h2_ref)
    for c in range(D_FF // FF_CHUNK):
        cs = slice(c * FF_CHUNK, (c + 1) * FF_CHUNK)
        _ffn_chunk(h2_ref, wg_ref[:, cs].astype(BF16), wu_ref[:, cs].astype(BF16), wd_ref[cs, :].astype(BF16), o_ref)


def _post_specs(T, n_seq, qm_block, layer):
    tiles_per_seq = (T // n_seq) // TOK_TILE
    in_specs = [
        pl.BlockSpec((TOK_TILE, D_MODEL), lambda i: (i, 0)),
        pl.BlockSpec((TOK_TILE, MEM_WIDTH), lambda i: (i, qm_block)),
        pl.BlockSpec((None, MEM_TOKENS, MEM_WIDTH), lambda i: (i // tiles_per_seq, 0, 0)),
        pl.BlockSpec((None, MEM_TOKENS, MEM_WIDTH), lambda i: (i // tiles_per_seq, 0, 0)),
        _layer_spec((D_MODEL, D_MODEL), layer),
        _const_spec((1, D_MODEL)),
        _layer_spec((D_MODEL, D_FF), layer),
        _layer_spec((D_MODEL, D_FF), layer),
        _layer_spec((D_FF, D_MODEL), layer),
    ]
    scratch = [pltpu.VMEM((TOK_TILE, MEM_WIDTH), BF16), pltpu.VMEM((TOK_TILE, D_MODEL), BF16)]
    return tiles_per_seq, in_specs, scratch


def _post_a(x2, tok2, qm2, km, vm, wo, g2, wg, wu, wd, qm_block, layer):
    T = x2.shape[0]
    _, in_specs, scratch = _post_specs(T, km.shape[0], qm_block, layer)
    in_specs.insert(1, pl.BlockSpec((TOK_TILE, TOK_WIDTH), lambda i: (i, 0)))
    return pl.pallas_call(
        _post_a_kernel,
        grid=(T // TOK_TILE,),
        in_specs=in_specs,
        out_specs=pl.BlockSpec((TOK_TILE, D_MODEL), lambda i: (i, 0)),
        out_shape=jax.ShapeDtypeStruct((T, D_MODEL), F32),
        scratch_shapes=scratch,
        compiler_params=pltpu.CompilerParams(dimension_semantics=("arbitrary",), vmem_limit_bytes=VMEM_LIMIT),
        name="post_a",
    )(x2, tok2, qm2, km, vm, wo, g2, wg, wu, wd)


def _row(v):
    return v.reshape(1, -1).astype(F32)


def _tile_heads(g, width, scale=1.0):
    return _row(jnp.tile(g * scale, width // HEAD_DIM))


def kernel(x, mem, norm1_g, mem_norm_g, a_w_in, a_q_g, a_k_g, a_rel_bias, b_w_in, b_b_in, b_conv_w, b_conv_b,
           b_ln_g, b_ln_b, mq_g, mk_g, w_mem_kv, w_out, norm2_g, w_gate, w_up, w_down):
    B, S, D = x.shape
    T = B * S
    depth = norm1_g.shape[0]
    x2 = x.reshape(T, D)
    for i in range(depth):
        j = i // 2
        km, vm = _mem_kv(mem, _row(mem_norm_g[i]), w_mem_kv, _tile_heads(mk_g[i], MEM_WIDTH), i)
        gm = _tile_heads(mq_g[i], MEM_WIDTH, ATTN_SCALE * LOG2E)
        ff = (w_out, _row(norm2_g[i]), w_gate, w_up, w_down)
        if i % 2 == 0:
            z = _in_proj_a(x2, _row(norm1_g[i]), a_w_in[j],
                           _tile_heads(a_q_g[j], COL_CHUNK, ATTN_SCALE * LOG2E), _tile_heads(a_k_g[j], COL_CHUNK), gm)
            tok = _band_attention(z.reshape(B, S, A_IN), _band_bias(a_rel_bias[j])).reshape(T, TOK_WIDTH)
            x2 = _post_a(x2, tok, z, km, vm, *ff, (3 * TOK_WIDTH) // MEM_WIDTH, i)
        else:
            h, qm2 = _in_proj_b(x2, _row(norm1_g[i]), b_w_in[j], _row(b_b_in[j]), gm)
            conv_w = jnp.broadcast_to(b_conv_w[j].astype(F32)[:, None, :], (CONV_WIDTH, SUBLANES, CONV_CH))
            tok = _conformer_conv(h.reshape(B, S, CONV_CH), conv_w, _row(b_conv_b[j]),
                                  _row(b_ln_g[j]), _row(b_ln_b[j])).reshape(T, CONV_CH)
            x2 = _post_a(x2, tok, qm2, km, vm, *ff, 0, i)
    return x2.reshape(B, S, D)
```
